```python
import jax, jax.numpy as jnp
from jax import lax
import numpy as np

D_MODEL = 1024
BATCH = 8
SEQ = 2048
DEPTH = 4

GRID_W = 64
HEAD_DIM = 64
EPS = 1e-6
A_HEADS = D_MODEL // 256
A_PATTERNS = ((128, 1), (512, 4), (2048, 16))
A_ROT_DIM = HEAD_DIM // 4
ROPE_THETA = 500000.0
B_HEADS = D_MODEL // 256
B_DK = 128
B_DV = 128
B_CONV = 5
B_CHUNK = 64
C_Q_HEADS = D_MODEL // 256
C_KV_HEADS = C_Q_HEADS // 2
C_THETA = 10000.0
Q_BLOCK = 128
D_FF = (-(-(8 * D_MODEL) // (3 * 256))) * 256

SPLITS = (
    A_HEADS * HEAD_DIM, A_HEADS * HEAD_DIM, A_HEADS * HEAD_DIM,
    B_HEADS * B_DK, B_HEADS * B_DK, B_HEADS * B_DV, B_HEADS * B_DV,
    2 * B_HEADS, 2 * B_HEADS,
    C_Q_HEADS * HEAD_DIM, C_KV_HEADS * HEAD_DIM, C_KV_HEADS * HEAD_DIM,
)
IN_DIM = sum(SPLITS)
D_MIX = A_HEADS * HEAD_DIM + B_HEADS * B_DV + C_Q_HEADS * HEAD_DIM
B_QKV = 2 * B_HEADS * B_DK + B_HEADS * B_DV

kernel_name = 'hybrid_parallel_dilated_gdn_axialgqa_block'

F32 = jnp.float32


def rmsnorm(x, w):
    xf = x.astype(F32)
    y = xf * lax.rsqrt(jnp.mean(xf * xf, axis=-1, keepdims=True) + EPS)
    return (y * w.astype(F32)).astype(x.dtype)


def l2norm(x):
    return x * lax.rsqrt(jnp.sum(x * x, axis=-1, keepdims=True) + EPS)


def rope(x, pos, theta):
    half = x.shape[-1] // 2
    inv = jnp.float32(theta) ** (-jnp.arange(half, dtype=F32) / half)
    ang = pos.astype(F32)[:, None] * inv[None, :]
    cos = jnp.cos(ang)[None, :, None, :]
    sin = jnp.sin(ang)[None, :, None, :]
    xf = x.astype(F32)
    x1, x2 = xf[..., :half], xf[..., half:]
    return jnp.concatenate([x1 * cos - x2 * sin, x2 * cos + x1 * sin], axis=-1).astype(x.dtype)


def to_strided(t, d):
    b, s = t.shape[:2]
    rest = t.shape[2:]
    return t.reshape(b, s // d, d, *rest).swapaxes(1, 2).reshape(b * d, s // d, *rest)


def from_strided(t, d, b):
    n, L = t.shape[:2]
    rest = t.shape[2:]
    return t.reshape(b, d, L, *rest).swapaxes(1, 2).reshape(b, L * d, *rest)


def banded_attention(q, k, v, radius):
    n, L, h, dh = q.shape
    blk = radius
    nb = -(-L // blk)
    pad = nb * blk - L
    qb = jnp.pad(q, ((0, 0), (0, pad), (0, 0), (0, 0))).reshape(n, nb, blk, h, dh)
    padk = ((0, 0), (blk, pad + blk), (0, 0), (0, 0))
    kp = jnp.pad(k, padk).reshape(n, nb + 2, blk, h, dh)
    vp = jnp.pad(v, padk).reshape(n, nb + 2, blk, h, dh)
    kw = jnp.concatenate([kp[:, :-2], kp[:, 1:-1], kp[:, 2:]], axis=2)
    vw = jnp.concatenate([vp[:, :-2], vp[:, 1:-1], vp[:, 2:]], axis=2)
    qpos = jnp.arange(nb)[:, None] * blk + jnp.arange(blk)[None, :]
    kpos = (jnp.arange(nb)[:, None] - 1) * blk + jnp.arange(3 * blk)[None, :]
    dist = qpos[:, :, None] - kpos[:, None, :]
    kp_b = kpos[:, None, :]
    valid = ((jnp.abs(dist) <= radius) & (kp_b >= 0) & (kp_b < L)) | (dist == 0)
    sc = jnp.einsum('nbqhd,nbkhd->nbhqk', qb.astype(F32), kw.astype(F32)) * (dh ** -0.5)
    sc = jnp.where(valid[None, :, None], sc, -jnp.inf)
    m = jnp.max(sc, axis=-1, keepdims=True)
    p = jnp.exp(sc - m)
    l = jnp.sum(p, axis=-1, keepdims=True)
    o = jnp.einsum('nbhqk,nbkhd->nbqhd', p / l, vw.astype(F32)).reshape(n, nb * blk, h, dh)[:, :L]
    lse = (m + jnp.log(l))[..., 0].transpose(0, 1, 3, 2).reshape(n, nb * blk, h)[:, :L]
    return o, lse


def mixer_dilated(q, k, v, qn, kn, pos):
    b, s = q.shape[:2]
    q = rmsnorm(q.reshape(b, s, A_HEADS, HEAD_DIM), qn)
    k = rmsnorm(k.reshape(b, s, A_HEADS, HEAD_DIM), kn)
    v = v.reshape(b, s, A_HEADS, HEAD_DIM)
    q = jnp.concatenate([rope(q[..., :A_ROT_DIM], pos, ROPE_THETA), q[..., A_ROT_DIM:]], axis=-1)
    k = jnp.concatenate([rope(k[..., :A_ROT_DIM], pos, ROPE_THETA), k[..., A_ROT_DIM:]], axis=-1)
    outs, lses = [], []
    for window, dil in A_PATTERNS:
        radius = window // (2 * dil)
        o, lse = banded_attention(to_strided(q, dil), to_strided(k, dil), to_strided(v, dil), radius)
        outs.append(from_strided(o, dil, b))
        lses.append(from_strided(lse, dil, b))
    wts = jax.nn.softmax(jnp.stack(lses), axis=0)
    o = jnp.einsum('pbsh,pbshd->bshd', wts, jnp.stack(outs))
    return o.reshape(b, s, -1).astype(v.dtype)


def short_conv(x, w):
    kk, c = w.shape
    return lax.conv_general_dilated(
        x, w[:, None, :].astype(x.dtype), window_strides=(1,),
        padding=[(kk // 2, kk // 2)], dimension_numbers=('NWC', 'WIO', 'NWC'),
        feature_group_count=c)


def gated_delta_chunked(q, k, v, g, beta):
    b, s, h, dk = q.shape
    dv = v.shape[-1]
    c = B_CHUNK
    n = s // c
    to_c = lambda t: t.reshape(b, n, c, h, t.shape[-1]).transpose(1, 0, 3, 2, 4)
    q, k, v = to_c(q), to_c(k), to_c(v)
    g = g.reshape(b, n, c, h).transpose(1, 0, 3, 2)
    beta = beta.reshape(b, n, c, h).transpose(1, 0, 3, 2)
    gc = jnp.cumsum(g, axis=-1)
    tril = jnp.tril(jnp.ones((c, c), dtype=bool))
    decay = jnp.exp(jnp.where(tril, gc[..., :, None] - gc[..., None, :], -jnp.inf))
    kbeta = k * beta[..., None]
    eye = jnp.eye(c, dtype=F32)
    a_mat = jnp.einsum('nbhid,nbhjd->nbhij', kbeta, k) * decay * (1.0 - eye)
    rhs = jnp.concatenate([v * beta[..., None], kbeta * jnp.exp(gc)[..., None]], axis=-1)
    sol = lax.linalg.triangular_solve(a_mat + eye, rhs, left_side=True, lower=True)
    u, w = sol[..., :dv], sol[..., dv:]
    qk = jnp.einsum('nbhid,nbhjd->nbhij', q, k) * decay

    def step(state, xs):
        q_i, k_i, u_i, w_i, gc_i, qk_i = xs
        v_new = u_i - jnp.einsum('bhck,bhkv->bhcv', w_i, state)
        o = (jnp.einsum('bhck,bhkv->bhcv', q_i * jnp.exp(gc_i)[..., None], state)
             + jnp.einsum('bhij,bhjv->bhiv', qk_i, v_new))
        g_last = gc_i[..., -1:]
        state = (state * jnp.exp(g_last)[..., None]
                 + jnp.einsum('bhck,bhcv->bhkv', k_i * jnp.exp(g_last - gc_i)[..., None], v_new))
        return state, o

    state0 = jnp.zeros((b, h, dk, dv), F32)
    _, o = lax.scan(step, state0, (q, k, u, w, gc, qk))
    return o.transpose(1, 0, 3, 2, 4).reshape(b, s, h, dv)


def mixer_gdn(q, k, v, z, a, beta_in, conv_w, a_log, dt_bias, onorm):
    b, s = q.shape[:2]
    qkv = jax.nn.silu(short_conv(jnp.concatenate([q, k, v], axis=-1), conv_w)).astype(F32)
    q, k, v = jnp.split(qkv, [B_HEADS * B_DK, 2 * B_HEADS * B_DK], axis=-1)
    q = l2norm(q.reshape(b, s, B_HEADS, B_DK)) * (B_DK ** -0.5)
    k = l2norm(k.reshape(b, s, B_HEADS, B_DK))
    v = v.reshape(b, s, B_HEADS, B_DV)
    a = a.astype(F32).reshape(b, s, 2, B_HEADS)
    beta = jax.nn.sigmoid(beta_in.astype(F32).reshape(b, s, 2, B_HEADS))
    g = -jnp.exp(a_log.astype(F32)) * jax.nn.softplus(a + dt_bias.astype(F32))
    o_f = gated_delta_chunked(q, k, v, g[:, :, 0], beta[:, :, 0])
    flip = lambda t: jnp.flip(t, axis=1)
    o_b = flip(gated_delta_chunked(flip(q), flip(k), flip(v), flip(g[:, :, 1]), flip(beta[:, :, 1])))
    o = rmsnorm(o_f + o_b, onorm) * jax.nn.silu(z.astype(F32).reshape(b, s, B_HEADS, B_DV))
    return o.reshape(b, s, -1).astype(z.dtype)


def mixer_gqa(q, k, v, qn, kn, row_pos, col_pos):
    b, s = q.shape[:2]
    q = rmsnorm(q.reshape(b, s, C_Q_HEADS, HEAD_DIM), qn)
    k = rmsnorm(k.reshape(b, s, C_KV_HEADS, HEAD_DIM), kn)
    v = v.reshape(b, s, C_KV_HEADS, HEAD_DIM)
    half = HEAD_DIM // 2
    axial = lambda t: jnp.concatenate(
        [rope(t[..., :half], row_pos, C_THETA), rope(t[..., half:], col_pos, C_THETA)], axis=-1)
    q, k = axial(q), axial(k)
    grp = C_Q_HEADS // C_KV_HEADS
    nblk = s // Q_BLOCK
    qb = q.reshape(b, nblk, Q_BLOCK, C_KV_HEADS, grp, HEAD_DIM).transpose(1, 0, 2, 3, 4, 5).astype(F32)
    kf, vf = k.astype(F32), v.astype(F32)

    def block(qi):
        sc = jnp.einsum('bqhgd,bshd->bhgqs', qi, kf) * (HEAD_DIM ** -0.5)
        p = jax.nn.softmax(sc, axis=-1)
        return jnp.einsum('bhgqs,bshd->bqhgd', p, vf)

    o = lax.map(block, qb)
    return o.transpose(1, 0, 2, 3, 4, 5).reshape(b, s, -1).astype(v.dtype)


def setup_inputs(seed: int = 0) -> dict:
    key = jax.random.key(seed)
    ks = jax.random.split(key, 16)
    nrm = lambda k, shape, fan: jax.random.normal(k, shape, F32) * (fan ** -0.5)
    gain = lambda k, shape: 1.0 + 0.01 * jax.random.normal(k, shape, F32)
    dt = jnp.exp(jax.random.uniform(ks[7], (DEPTH, 2, B_HEADS), F32, np.log(1e-3), np.log(1e-1)))
    return {
        'x': jax.random.normal(ks[0], (BATCH, SEQ, D_MODEL), F32),
        'norm1': gain(ks[1], (DEPTH, D_MODEL)),
        'w_in': nrm(ks[2], (DEPTH, D_MODEL, IN_DIM), D_MODEL),
        'qn_a': gain(ks[3], (DEPTH, HEAD_DIM)),
        'kn_a': gain(ks[4], (DEPTH, HEAD_DIM)),
        'conv_b': nrm(ks[5], (DEPTH, B_CONV, B_QKV), B_CONV),
        'a_log_b': jnp.log(jax.random.uniform(ks[6], (DEPTH, 2, B_HEADS), F32, 1.0, 16.0)),
        'dt_bias_b': jnp.log(jnp.expm1(dt)),
        'onorm_b': gain(ks[8], (DEPTH, B_DV)),
        'qn_c': gain(ks[9], (DEPTH, HEAD_DIM)),
        'kn_c': gain(ks[10], (DEPTH, HEAD_DIM)),
        'w_out': nrm(ks[11], (DEPTH, D_MIX, D_MODEL), D_MIX),
        'norm2': gain(ks[12], (DEPTH, D_MODEL)),
        'w_gate_up': nrm(ks[13], (DEPTH, D_MODEL, 2 * D_FF), D_MODEL),
        'w_down': nrm(ks[14], (DEPTH, D_FF, D_MODEL), D_FF),
    }


def reference(x, norm1, w_in, qn_a, kn_a, conv_b, a_log_b, dt_bias_b, onorm_b, qn_c, kn_c,
              w_out, norm2, w_gate_up, w_down):
    s = x.shape[1]
    rows = s // GRID_W
    pos = jnp.arange(s)
    row_pos = jnp.repeat(jnp.arange(rows), GRID_W)
    col_pos = jnp.tile(jnp.arange(GRID_W), rows)
    cuts = np.cumsum(SPLITS)[:-1].tolist()
    for i in range(DEPTH):
        h = rmsnorm(x, norm1[i])
        (qa, ka, va, qb, kb, vb, zb, ab, bb, qc, kc, vc) = jnp.split(h @ w_in[i], cuts, axis=-1)
        o_a = mixer_dilated(qa, ka, va, qn_a[i], kn_a[i], pos)
        o_b = mixer_gdn(qb, kb, vb, zb, ab, bb, conv_b[i], a_log_b[i], dt_bias_b[i], onorm_b[i])
        o_c = mixer_gqa(qc, kc, vc, qn_c[i], kn_c[i], row_pos, col_pos)
        x = x + jnp.concatenate([o_a, o_b, o_c], axis=-1) @ w_out[i]
        gate, up = jnp.split(rmsnorm(x, norm2[i]) @ w_gate_up[i], 2, axis=-1)
        x = x + (jax.nn.silu(gate) * up) @ w_down[i]
    return x
```

```python
import functools

import numpy as np
import jax
import jax.numpy as jnp
from jax import lax
from jax.experimental import pallas as pl
from jax.experimental.pallas import tpu as pltpu

F32 = jnp.float32
BF16 = jnp.bfloat16

EPS = 1e-6
GRID_W = 64
HEAD_DIM = 64
N_HEADS = 4
A_PATTERNS = ((128, 1), (512, 4), (2048, 16))
A_ROT_HALF = 8
A_THETA = 500000.0
C_ROT_HALF = 16
C_THETA = 10000.0
B_DK = 128
B_CONV = 5
CHUNK = 64
NEG = -1e30

W_ATT = N_HEADS * HEAD_DIM
W_GDN = N_HEADS * B_DK
COL_BQ, COL_BK, COL_BV, COL_BZ = 0, 512, 1024, 1536
COL_AQ, COL_AK, COL_AV = 2048, 2304, 2560
COL_CQ, COL_CK, COL_CV = 2816, 3072, 3200
COL_AB = 3328
IN_PAD = 3456
C_HEAD_ORDER = (0, 2, 1, 3)

VMEM_LIMIT = 56 * 1024 * 1024


def _cparams(sem):
    return pltpu.CompilerParams(dimension_semantics=sem, vmem_limit_bytes=VMEM_LIMIT)


def _split2(x):
    hi = x.astype(BF16)
    lo = (x - hi.astype(F32)).astype(BF16)
    return hi, lo


def _dot(a, b):
    return jnp.dot(a, b, preferred_element_type=F32)


def _dot_nt(a, b):
    return lax.dot_general(a, b, (((1,), (1,)), ((), ())), preferred_element_type=F32)


def _dot_tn(a, b):
    return lax.dot_general(a, b, (((0,), (0,)), ((), ())), preferred_element_type=F32)


def _dot3(a, b):
    ah, al = _split2(a)
    bh, bl = _split2(b)
    return _dot(ah, bh) + (_dot(ah, bl) + _dot(al, bh))


def _sigmoid(x):
    return 1.0 / (1.0 + jnp.exp(-x))


def _inproj_kernel(x_ref, nw_ref, w_ref, o_ref):
    x = x_ref[...]
    ms = jnp.mean(x * x, axis=-1, keepdims=True)
    y = (x * lax.rsqrt(ms + EPS)) * nw_ref[...]
    o_ref[...] = _dot(y.astype(BF16), w_ref[...])


def _in_proj(x, nw, w, tm=256):
    t, d = x.shape
    n = w.shape[1]
    return pl.pallas_call(
        _inproj_kernel,
        grid=(t // tm,),
        in_specs=[pl.BlockSpec((tm, d), lambda i: (i, 0)),
                  pl.BlockSpec((1, d), lambda i: (0, 0)),
                  pl.BlockSpec((d, n), lambda i: (0, 0))],
        out_specs=pl.BlockSpec((tm, n), lambda i: (i, 0)),
        out_shape=jax.ShapeDtypeStruct((t, n), F32),
        compiler_params=_cparams(("parallel",)),
        name="in_proj",
    )(x, nw, w)


def _head_norm_rope(x, w, e_hi, cos, sin_lo, sin_hi, rot_half, scale):
    xx = x * x
    hi, lo = _split2(xx)
    ms = _dot(hi, e_hi) + _dot(lo, e_hi)
    y = (x * lax.rsqrt(ms + EPS)) * w
    width = x.shape[1]
    up = pltpu.roll(y, width - rot_half, 1)
    dn = pltpu.roll(y, rot_half, 1)
    out = y * cos + up * sin_lo + dn * sin_hi
    return out * scale


def _attn_prep_kernel(aq_ref, ak_ref, av_ref, cq_ref, ck_ref, cv_ref,
                      qna_ref, kna_ref, qnc_ref, knc_ref, ta_ref, tc_ref, e_ref,
                      oqa_ref, oka_ref, ova_ref, oqc_ref, okc_ref, ovc_ref):
    e = e_ref[...]
    cos_a, sl_a, sh_a = ta_ref[0], ta_ref[1], ta_ref[2]
    cos_c, sl_c, sh_c = tc_ref[0], tc_ref[1], tc_ref[2]
    scale = HEAD_DIM ** -0.5
    oqa_ref[...] = _head_norm_rope(aq_ref[...], qna_ref[...], e, cos_a, sl_a, sh_a, A_ROT_HALF, scale).astype(BF16)
    oka_ref[...] = _head_norm_rope(ak_ref[...], kna_ref[...], e, cos_a, sl_a, sh_a, A_ROT_HALF, 1.0).astype(BF16)
    ova_ref[...] = av_ref[...].astype(BF16)
    oqc_ref[...] = _head_norm_rope(cq_ref[...], qnc_ref[...], e, cos_c, sl_c, sh_c, C_ROT_HALF, scale).astype(BF16)
    okc_ref[...] = _head_norm_rope(ck_ref[...], knc_ref[...][:, :128], e[:128, :128],
                                   cos_c[:, :128], sl_c[:, :128], sh_c[:, :128], C_ROT_HALF, 1.0).astype(BF16)
    ovc_ref[...] = cv_ref[...].astype(BF16)


def _attn_prep(proj, qna, kna, qnc, knc, tab_a, tab_c, e_mat, seq, tm=512):
    t = proj.shape[0]
    ns = seq // tm
    cb = lambda col, w: pl.BlockSpec((tm, w), lambda i, c=col // w: (i, c))
    vec = pl.BlockSpec((1, W_ATT), lambda i: (0, 0))
    tab = pl.BlockSpec((3, tm, W_ATT), lambda i: (0, i % ns, 0))
    ob = lambda w: pl.BlockSpec((tm, w), lambda i: (i, 0))
    return pl.pallas_call(
        _attn_prep_kernel,
        grid=(t // tm,),
        in_specs=[cb(COL_AQ, 256), cb(COL_AK, 256), cb(COL_AV, 256),
                  cb(COL_CQ, 256), cb(COL_CK, 128), cb(COL_CV, 128),
                  vec, vec, vec, vec, tab, tab,
                  pl.BlockSpec((W_ATT, W_ATT), lambda i: (0, 0))],
        out_specs=[ob(256), ob(256), ob(256), ob(256), ob(128), ob(128)],
        out_shape=[jax.ShapeDtypeStruct((t, w), BF16) for w in (256, 256, 256, 256, 128, 128)],
        compiler_params=_cparams(("parallel",)),
        name="attn_prep",
    )(proj, proj, proj, proj, proj, proj, qna, kna, qnc, knc, tab_a, tab_c, e_mat)


def _softmax_pv(s, v, mult):
    if mult is not None:
        s = jnp.where(mult > 0, s, NEG)
    m = jnp.max(s, axis=-1, keepdims=True)
    p = jnp.exp(s - m)
    if mult is not None:
        p = p * mult.astype(F32)
    l = jnp.sum(p, axis=-1, keepdims=True)
    return _dot(p.astype(BF16), v), l


def _attn_a_kernel(q_ref, k_ref, v_ref, m_ref, o_ref):
    q = q_ref[...]
    k = k_ref[...]
    v = v_ref[...]
    mult = m_ref[...]
    lane = lax.broadcasted_iota(jnp.int32, (1, W_ATT), 1)
    acc = jnp.zeros(q.shape, F32)
    for h in range(N_HEADS):
        hm = (lane >= h * HEAD_DIM) & (lane < (h + 1) * HEAD_DIM)
        qh = jnp.where(hm, q, jnp.zeros_like(q))
        pv, l = _softmax_pv(_dot_nt(qh, k), v, mult)
        acc = acc + jnp.where(hm, pv * (1.0 / l), 0.0)
    o_ref[...] = acc.astype(o_ref.dtype)


def _attn_a(q, k, v, mult, batch, seq, tq=256):
    nq = seq // tq
    return pl.pallas_call(
        _attn_a_kernel,
        grid=(nq, batch),
        in_specs=[pl.BlockSpec((tq, W_ATT), lambda i, b: (b * nq + i, 0)),
                  pl.BlockSpec((seq, W_ATT), lambda i, b: (b, 0)),
                  pl.BlockSpec((seq, W_ATT), lambda i, b: (b, 0)),
                  pl.BlockSpec((tq, seq), lambda i, b: (i, 0))],
        out_specs=pl.BlockSpec((tq, W_ATT), lambda i, b: (b * nq + i, 0)),
        out_shape=jax.ShapeDtypeStruct((batch * seq, W_ATT), BF16),
        compiler_params=_cparams(("parallel", "parallel")),
        name="attn_a",
    )(q, k, v, mult)


def _attn_c_kernel(q_ref, k_ref, v_ref, o_ref):
    k = k_ref[...]
    v = v_ref[...]
    lane = lax.broadcasted_iota(jnp.int32, (1, 128), 1)
    for pair in range(2):
        q = q_ref[:, pair * 128:(pair + 1) * 128]
        acc = jnp.zeros(q.shape, F32)
        for half in range(2):
            hm = (lane >= half * HEAD_DIM) & (lane < (half + 1) * HEAD_DIM)
            qh = jnp.where(hm, q, jnp.zeros_like(q))
            pv, l = _softmax_pv(_dot_nt(qh, k), v, None)
            acc = acc + jnp.where(hm, pv * (1.0 / l), 0.0)
        o_ref[:, pair * 128:(pair + 1) * 128] = acc.astype(o_ref.dtype)


def _attn_c(q, k, v, batch, seq, tq=256):
    nq = seq // tq
    return pl.pallas_call(
        _attn_c_kernel,
        grid=(batch, nq),
        in_specs=[pl.BlockSpec((tq, W_ATT), lambda b, i: (b * nq + i, 0)),
                  pl.BlockSpec((seq, 128), lambda b, i: (b, 0)),
                  pl.BlockSpec((seq, 128), lambda b, i: (b, 0))],
        out_specs=pl.BlockSpec((tq, W_ATT), lambda b, i: (b * nq + i, 0)),
        out_shape=jax.ShapeDtypeStruct((batch * seq, W_ATT), BF16),
        compiler_params=_cparams(("parallel", "parallel")),
        name="attn_c",
    )(q, k, v)


def _gdn_prep_kernel(x_ref, w_ref, o_ref):
    j = pl.program_id(1)
    x = x_ref[...]
    w = w_ref[...]
    seq = x.shape[0]
    pos = lax.broadcasted_iota(jnp.int32, (seq, 1), 0)
    acc = x * w[B_CONV // 2:B_CONV // 2 + 1, :]
    for tap in range(B_CONV):
        off = tap - B_CONV // 2
        if off == 0:
            continue
        sh = pltpu.roll(x, (-off) % seq, 0)
        ok = (pos + off >= 0) & (pos + off < seq)
        acc = acc + jnp.where(ok, sh, 0.0) * w[tap:tap + 1, :]
    y = acc * _sigmoid(acc)
    norm_scale = jnp.where(j == 0, B_DK ** -0.5, 1.0).astype(F32)
    for h in range(N_HEADS):
        yh = y[:, h * B_DK:(h + 1) * B_DK]
        ss = jnp.sum(yh * yh, axis=-1, keepdims=True)
        nh = yh * (lax.rsqrt(ss + EPS) * norm_scale)
        o_ref[:, h * B_DK:(h + 1) * B_DK] = jnp.where(j < 2, nh, yh).astype(o_ref.dtype)


def _gdn_prep(proj, conv_w, batch, seq):
    return pl.pallas_call(
        _gdn_prep_kernel,
        grid=(batch, 3),
        in_specs=[pl.BlockSpec((seq, W_GDN), lambda b, j: (b, j)),
                  pl.BlockSpec((B_CONV, W_GDN), lambda b, j: (0, j))],
        out_specs=pl.BlockSpec((seq, W_GDN), lambda b, j: (b, j)),
        out_shape=jax.ShapeDtypeStruct((batch * seq, 3 * W_GDN), BF16),
        compiler_params=_cparams(("parallel", "parallel")),
        name="gdn_prep",
    )(proj, conv_w)


def _unit_tri_inverse(a, eye):
    p = eye - a
    ap = _dot3(a, a)
    for lvl in range(5):
        p = p + _dot3(p, ap)
        if lvl < 4:
            ap = _dot3(ap, ap)
    return p


def _gdn_step(n, rev, qkv_ref, ab_ref, alog, dtb, tri_ref, s_ref, osc_ref):
    rows = pl.ds(pl.multiple_of(n * CHUNK, CHUNK), CHUNK)
    ab = ab_ref[rows, :]
    xg = ab + dtb
    softplus = jnp.maximum(xg, 0.0) + jnp.log(1.0 + jnp.exp(-jnp.abs(xg)))
    g = -jnp.exp(alog) * softplus
    beta = _sigmoid(ab)
    tri = tri_ref[1 if rev else 0]
    g1 = g.astype(BF16)
    r1 = g - g1.astype(F32)
    g2 = r1.astype(BF16)
    g3 = (r1 - g2.astype(F32)).astype(BF16)
    gc = _dot(tri, g1) + (_dot(tri, g2) + _dot(tri, g3))
    g_last = gc[0:1, :] if rev else gc[CHUNK - 1:CHUNK, :]
    eg = jnp.exp(gc)
    eg_last = jnp.exp(g_last)
    e_rest = jnp.exp(g_last - gc)
    gc_t = jnp.transpose(jnp.concatenate([gc, jnp.zeros_like(gc)], axis=0))
    ii = lax.broadcasted_iota(jnp.int32, (CHUNK, CHUNK), 0)
    jj = lax.broadcasted_iota(jnp.int32, (CHUNK, CHUNK), 1)
    incl = (ii <= jj) if rev else (ii >= jj)
    strict = (ii < jj) if rev else (ii > jj)
    eye = jnp.where(ii == jj, 1.0, 0.0).astype(F32)
    for h in range(N_HEADS):
        c = (N_HEADS if rev else 0) + h
        qh = qkv_ref[rows, h * B_DK:(h + 1) * B_DK].astype(F32)
        kh = qkv_ref[rows, W_GDN + h * B_DK:W_GDN + (h + 1) * B_DK].astype(F32)
        vh = qkv_ref[rows, 2 * W_GDN + h * B_DK:2 * W_GDN + (h + 1) * B_DK].astype(F32)
        bcol = beta[:, 8 + c:9 + c]
        egc = eg[:, c:c + 1]
        diff = gc[:, c:c + 1] - gc_t[c:c + 1, 0:CHUNK]
        dec = jnp.exp(jnp.where(incl, diff, NEG))
        kb = kh * bcol
        kq = _dot_nt(jnp.concatenate([kb, qh], axis=0).astype(BF16), kh.astype(BF16))
        a_mat = jnp.where(strict, kq[0:CHUNK] * dec, 0.0)
        qk = kq[CHUNK:2 * CHUNK] * dec
        t_inv = _unit_tri_inverse(a_mat, eye)
        th, tl = _split2(t_inv)
        rhs = jnp.concatenate([vh * bcol, kb * egc], axis=1).astype(BF16)
        uw = _dot(th, rhs) + _dot(tl, rhs)
        u = uw[:, 0:B_DK]
        w = uw[:, B_DK:2 * B_DK]
        state = s_ref[c]
        wq = _dot(jnp.concatenate([w, qh * egc], axis=0).astype(BF16), state.astype(BF16))
        v_new = u - wq[0:CHUNK]
        o = wq[CHUNK:2 * CHUNK] + _dot(qk.astype(BF16), v_new.astype(BF16))
        kd = (kh * e_rest[:, c:c + 1]).astype(BF16)
        s_ref[c] = state * eg_last[:, c:c + 1] + _dot_tn(kd, v_new.astype(BF16))
        osc_ref[rows, h * B_DK:(h + 1) * B_DK] = o


def _gdn_kernel(qkv_ref, ab_ref, z_ref, alog_ref, dtb_ref, onorm_ref, tri_ref, o_ref, s_ref, of_ref, ob_ref):
    seq = qkv_ref.shape[0]
    nc = seq // CHUNK
    s_ref[...] = jnp.zeros(s_ref.shape, F32)
    alog = alog_ref[...]
    dtb = dtb_ref[...]

    def body(n, carry):
        _gdn_step(n, False, qkv_ref, ab_ref, alog, dtb, tri_ref, s_ref, of_ref)
        _gdn_step(nc - 1 - n, True, qkv_ref, ab_ref, alog, dtb, tri_ref, s_ref, ob_ref)
        return carry

    lax.fori_loop(0, nc, body, 0)

    tile = 256

    def epilogue(i, carry):
        rows = pl.ds(pl.multiple_of(i * tile, tile), tile)
        for h in range(N_HEADS):
            cols = slice(h * B_DK, (h + 1) * B_DK)
            o = of_ref[rows, cols] + ob_ref[rows, cols]
            ms = jnp.mean(o * o, axis=-1, keepdims=True)
            y = (o * lax.rsqrt(ms + EPS)) * onorm_ref[...]
            z = z_ref[rows, cols]
            o_ref[rows, cols] = (y * (z * _sigmoid(z))).astype(o_ref.dtype)
        return carry

    lax.fori_loop(0, seq // tile, epilogue, 0)


def _gdn(qkv, proj, alog, dtb, onorm, tri, batch, seq):
    return pl.pallas_call(
        _gdn_kernel,
        grid=(batch,),
        in_specs=[pl.BlockSpec((seq, 3 * W_GDN), lambda b: (b, 0)),
                  pl.BlockSpec((seq, 128), lambda b: (b, COL_AB // 128)),
                  pl.BlockSpec((seq, W_GDN), lambda b: (b, COL_BZ // W_GDN)),
                  pl.BlockSpec((1, 128), lambda b: (0, 0)),
                  pl.BlockSpec((1, 128), lambda b: (0, 0)),
                  pl.BlockSpec((1, B_DK), lambda b: (0, 0)),
                  pl.BlockSpec((2, CHUNK, CHUNK), lambda b: (0, 0, 0))],
        out_specs=pl.BlockSpec((seq, W_GDN), lambda b: (b, 0)),
        out_shape=jax.ShapeDtypeStruct((batch * seq, W_GDN), BF16),
        scratch_shapes=[pltpu.VMEM((2 * N_HEADS, B_DK, B_DK), F32),
                        pltpu.VMEM((seq, W_GDN), F32),
                        pltpu.VMEM((seq, W_GDN), F32)],
        compiler_params=_cparams(("parallel",)),
        name="gdn",
    )(qkv, proj, proj, alog, dtb, onorm, tri)


def _outproj_kernel(x_ref, oa_ref, ob_ref, oc_ref, wa_ref, wb_ref, wc_ref, o_ref):
    acc = x_ref[...] + _dot(oa_ref[...], wa_ref[...])
    acc = acc + _dot(ob_ref[...], wb_ref[...])
    acc = acc + _dot(oc_ref[...], wc_ref[...])
    o_ref[...] = acc


def _out_proj(x, oa, ob, oc, wa, wb, wc, tm=512):
    t, d = x.shape
    row = lambda w: pl.BlockSpec((tm, w), lambda i: (i, 0))
    full = lambda w: pl.BlockSpec((w, d), lambda i: (0, 0))
    return pl.pallas_call(
        _outproj_kernel,
        grid=(t // tm,),
        in_specs=[row(d), row(W_ATT), row(W_GDN), row(W_ATT), full(W_ATT), full(W_GDN), full(W_ATT)],
        out_specs=row(d),
        out_shape=jax.ShapeDtypeStruct((t, d), F32),
        compiler_params=_cparams(("parallel",)),
        name="out_proj",
    )(x, oa, ob, oc, wa, wb, wc)


def _ffn_kernel(x_ref, nw_ref, wg_ref, wu_ref, wd_ref, o_ref, xn_ref, acc_ref):
    j = pl.program_id(1)

    @pl.when(j == 0)
    def _():
        x = x_ref[...]
        ms = jnp.mean(x * x, axis=-1, keepdims=True)
        xn_ref[...] = ((x * lax.rsqrt(ms + EPS)) * nw_ref[...]).astype(BF16)
        acc_ref[...] = x

    xn = xn_ref[...]
    gate = _dot(xn, wg_ref[...])
    up = _dot(xn, wu_ref[...])
    hidden = (gate * _sigmoid(gate)) * up
    acc_ref[...] += _dot(hidden.astype(BF16), wd_ref[...])

    @pl.when(j == pl.num_programs(1) - 1)
    def _():
        o_ref[...] = acc_ref[...]


def _ffn(x, nw, w_gate_up, w_down, tm=512, tf=1408):
    t, d = x.shape
    dff = w_down.shape[0]
    nf = dff // tf
    return pl.pallas_call(
        _ffn_kernel,
        grid=(t // tm, nf),
        in_specs=[pl.BlockSpec((tm, d), lambda i, j: (i, 0)),
                  pl.BlockSpec((1, d), lambda i, j: (0, 0)),
                  pl.BlockSpec((d, tf), lambda i, j: (0, j)),
                  pl.BlockSpec((d, tf), lambda i, j: (0, j + nf)),
                  pl.BlockSpec((tf, d), lambda i, j: (j, 0))],
        out_specs=pl.BlockSpec((tm, d), lambda i, j: (i, 0)),
        out_shape=jax.ShapeDtypeStruct((t, d), F32),
        scratch_shapes=[pltpu.VMEM((tm, d), BF16), pltpu.VMEM((tm, d), F32)],
        compiler_params=_cparams(("parallel", "arbitrary")),
        name="ffn",
    )(x, nw, w_gate_up, w_gate_up, w_down)


def _rope_tables(seq):
    t = np.arange(seq, dtype=np.float64)
    lane = np.arange(HEAD_DIM)

    def build(groups):
        cos = np.ones((seq, HEAD_DIM))
        s_lo = np.zeros((seq, HEAD_DIM))
        s_hi = np.zeros((seq, HEAD_DIM))
        for start, half, theta, pos in groups:
            inv = np.float64(np.float32(theta)) ** (-np.arange(half, dtype=np.float64) / half)
            ang = pos[:, None] * inv[None, :]
            cos[:, start:start + half] = np.cos(ang)
            cos[:, start + half:start + 2 * half] = np.cos(ang)
            s_lo[:, start:start + half] = -np.sin(ang)
            s_hi[:, start + half:start + 2 * half] = np.sin(ang)
        tab = np.stack([cos, s_lo, s_hi])
        return np.tile(tab, (1, 1, N_HEADS)).astype(np.float32)

    del lane
    tab_a = build([(0, A_ROT_HALF, A_THETA, t)])
    row = np.floor(t / GRID_W)
    col = t - row * GRID_W
    tab_c = build([(0, C_ROT_HALF, C_THETA, row), (2 * C_ROT_HALF, C_ROT_HALF, C_THETA, col)])
    return tab_a, tab_c


def _multiplicity(seq):
    d = np.arange(seq)[:, None] - np.arange(seq)[None, :]
    m = np.zeros((seq, seq), np.float32)
    for window, dil in A_PATTERNS:
        radius = window // (2 * dil)
        m += ((d % dil == 0) & (np.abs(d) <= radius * dil)).astype(np.float32)
    return m


def _head_mean_matrix():
    e = np.kron(np.eye(N_HEADS), np.full((HEAD_DIM, HEAD_DIM), 1.0 / HEAD_DIM))
    return e.astype(np.float32)


def _tri_operators():
    i = np.arange(CHUNK)
    lower = (i[:, None] >= i[None, :]).astype(np.float32)
    return np.stack([lower, lower.T])


def _pad_lanes(v, width):
    return jnp.pad(v.reshape(1, -1), ((0, 0), (0, width - v.size)))


def kernel(x, norm1, w_in, qn_a, kn_a, conv_b, a_log_b, dt_bias_b, onorm_b, qn_c, kn_c, w_out, norm2, w_gate_up, w_down):
    batch, seq, d = x.shape
    depth = w_in.shape[0]
    t = batch * seq

    tab_a, tab_c = _rope_tables(seq)
    tab_a, tab_c = jnp.asarray(tab_a), jnp.asarray(tab_c)
    mult = jnp.asarray(_multiplicity(seq), dtype=BF16)
    e_mat = jnp.asarray(_head_mean_matrix(), dtype=BF16)
    tri = jnp.asarray(_tri_operators(), dtype=BF16)

    c0 = 3 * W_ATT
    c1 = c0 + 4 * W_GDN
    c2 = c1 + 4 * N_HEADS
    cq = [w_in[:, :, c2 + h * HEAD_DIM:c2 + (h + 1) * HEAD_DIM] for h in C_HEAD_ORDER]
    w_in_p = jnp.concatenate(
        [w_in[:, :, c0:c1], w_in[:, :, 0:c0]] + cq + [w_in[:, :, c2 + W_ATT:], w_in[:, :, c1:c2]], axis=-1)
    w_in_p = jnp.pad(w_in_p, ((0, 0), (0, 0), (0, IN_PAD - w_in_p.shape[-1]))).astype(BF16)

    w_out_a = w_out[:, 0:W_ATT].astype(BF16)
    w_out_b = w_out[:, W_ATT:W_ATT + W_GDN].astype(BF16)
    wc = w_out[:, W_ATT + W_GDN:]
    w_out_c = jnp.concatenate([wc[:, h * HEAD_DIM:(h + 1) * HEAD_DIM] for h in C_HEAD_ORDER], axis=1).astype(BF16)
    w_gu = w_gate_up.astype(BF16)
    w_dn = w_down.astype(BF16)

    tile4 = lambda v: jnp.tile(v.reshape(1, -1), (1, N_HEADS))

    xt = x.reshape(t, d)
    for i in range(depth):
        proj = _in_proj(xt, norm1[i].reshape(1, d), w_in_p[i])
        qa, ka, va, qc, kc, vc = _attn_prep(proj, tile4(qn_a[i]), tile4(kn_a[i]), tile4(qn_c[i]), tile4(kn_c[i]),
                                            tab_a, tab_c, e_mat, seq)
        o_a = _attn_a(qa, ka, va, mult, batch, seq)
        o_c = _attn_c(qc, kc, vc, batch, seq)
        qkv_b = _gdn_prep(proj, conv_b[i], batch, seq)
        o_b = _gdn(qkv_b, proj, _pad_lanes(a_log_b[i], 128), _pad_lanes(dt_bias_b[i], 128),
                   onorm_b[i].reshape(1, B_DK), tri, batch, seq)
        xt = _out_proj(xt, o_a, o_b, o_c, w_out_a[i], w_out_b[i], w_out_c[i])
        xt = _ffn(xt, norm2[i].reshape(1, d), w_gu[i], w_dn[i])
    return xt.reshape(batch, seq, d)
```

```python
import functools

import numpy as np
import jax
import jax.numpy as jnp
from jax import lax
from jax.experimental import pallas as pl
from jax.experimental.pallas import tpu as pltpu

F32 = jnp.float32
BF16 = jnp.bfloat16

EPS = 1e-6
GRID_W = 64
HEAD_DIM = 64
N_HEADS = 4
A_PATTERNS = ((128, 1), (512, 4), (2048, 16))
A_ROT_HALF = 8
A_THETA = 500000.0
C_ROT_HALF = 16
C_THETA = 10000.0
B_DK = 128
B_CONV = 5
CHUNK = 64
NEG = -1e30

W_ATT = N_HEADS * HEAD_DIM
W_GDN = N_HEADS * B_DK
COL_BQ, COL_BK, COL_BV, COL_BZ = 0, 512, 1024, 1536
COL_AQ, COL_AK, COL_AV = 2048, 2304, 2560
COL_CQ, COL_CK, COL_CV = 2816, 3072, 3200
COL_AB = 3328
IN_PAD = 3456
C_HEAD_ORDER = (0, 2, 1, 3)

VMEM_LIMIT = 56 * 1024 * 1024


def _cparams(sem):
    return pltpu.CompilerParams(dimension_semantics=sem, vmem_limit_bytes=VMEM_LIMIT)


def _split2(x):
    hi = x.astype(BF16)
    lo = (x - hi.astype(F32)).astype(BF16)
    return hi, lo


def _dot(a, b):
    return jnp.dot(a, b, preferred_element_type=F32)


def _dot_nt(a, b):
    return lax.dot_general(a, b, (((1,), (1,)), ((), ())), preferred_element_type=F32)


def _dot_tn(a, b):
    return lax.dot_general(a, b, (((0,), (0,)), ((), ())), preferred_element_type=F32)


def _sigmoid(x):
    return 1.0 / (1.0 + jnp.exp(-x))


def _inproj_kernel(x_ref, nw_ref, w_ref, o_ref):
    x = x_ref[...]
    ms = jnp.mean(x * x, axis=-1, keepdims=True)
    y = (x * lax.rsqrt(ms + EPS)) * nw_ref[...]
    o_ref[...] = _dot(y.astype(BF16), w_ref[...])


def _in_proj(x, nw, w, tm=256):
    t, d = x.shape
    n = w.shape[1]
    return pl.pallas_call(
        _inproj_kernel,
        grid=(t // tm,),
        in_specs=[pl.BlockSpec((tm, d), lambda i: (i, 0)),
                  pl.BlockSpec((1, d), lambda i: (0, 0)),
                  pl.BlockSpec((d, n), lambda i: (0, 0))],
        out_specs=pl.BlockSpec((tm, n), lambda i: (i, 0)),
        out_shape=jax.ShapeDtypeStruct((t, n), F32),
        compiler_params=_cparams(("parallel",)),
        name="in_proj",
    )(x, nw, w)


def _head_norm_rope(x, w, e_hi, cos, sin_lo, sin_hi, rot_half, scale):
    xx = x * x
    hi, lo = _split2(xx)
    ms = _dot(hi, e_hi) + _dot(lo, e_hi)
    y = (x * lax.rsqrt(ms + EPS)) * w
    width = x.shape[1]
    up = pltpu.roll(y, width - rot_half, 1)
    dn = pltpu.roll(y, rot_half, 1)
    out = y * cos + up * sin_lo + dn * sin_hi
    return out * scale


def _attn_prep_kernel(aq_ref, ak_ref, av_ref, cq_ref, ck_ref, cv_ref,
                      qna_ref, kna_ref, qnc_ref, knc_ref, ta_ref, tc_ref, e_ref,
                      oqa_ref, oka_ref, ova_ref, oqc_ref, okc_ref, ovc_ref):
    e = e_ref[...]
    cos_a, sl_a, sh_a = ta_ref[0], ta_ref[1], ta_ref[2]
    cos_c, sl_c, sh_c = tc_ref[0], tc_ref[1], tc_ref[2]
    scale = HEAD_DIM ** -0.5
    oqa_ref[...] = _head_norm_rope(aq_ref[...], qna_ref[...], e, cos_a, sl_a, sh_a, A_ROT_HALF, scale).astype(BF16)
    oka_ref[...] = _head_norm_rope(ak_ref[...], kna_ref[...], e, cos_a, sl_a, sh_a, A_ROT_HALF, 1.0).astype(BF16)
    ova_ref[...] = av_ref[...].astype(BF16)
    oqc_ref[...] = _head_norm_rope(cq_ref[...], qnc_ref[...], e, cos_c, sl_c, sh_c, C_ROT_HALF, scale).astype(BF16)
    okc_ref[...] = _head_norm_rope(ck_ref[...], knc_ref[...][:, :128], e[:128, :128],
                                   cos_c[:, :128], sl_c[:, :128], sh_c[:, :128], C_ROT_HALF, 1.0).astype(BF16)
    ovc_ref[...] = cv_ref[...].astype(BF16)


def _attn_prep(proj, qna, kna, qnc, knc, tab_a, tab_c, e_mat, seq, tm=512):
    t = proj.shape[0]
    ns = seq // tm
    cb = lambda col, w: pl.BlockSpec((tm, w), lambda i, c=col // w: (i, c))
    vec = pl.BlockSpec((1, W_ATT), lambda i: (0, 0))
    tab = pl.BlockSpec((3, tm, W_ATT), lambda i: (0, i % ns, 0))
    ob = lambda w: pl.BlockSpec((tm, w), lambda i: (i, 0))
    return pl.pallas_call(
        _attn_prep_kernel,
        grid=(t // tm,),
        in_specs=[cb(COL_AQ, 256), cb(COL_AK, 256), cb(COL_AV, 256),
                  cb(COL_CQ, 256), cb(COL_CK, 128), cb(COL_CV, 128),
                  vec, vec, vec, vec, tab, tab,
                  pl.BlockSpec((W_ATT, W_ATT), lambda i: (0, 0))],
        out_specs=[ob(256), ob(256), ob(256), ob(256), ob(128), ob(128)],
        out_shape=[jax.ShapeDtypeStruct((t, w), BF16) for w in (256, 256, 256, 256, 128, 128)],
        compiler_params=_cparams(("parallel",)),
        name="attn_prep",
    )(proj, proj, proj, proj, proj, proj, qna, kna, qnc, knc, tab_a, tab_c, e_mat)


def _softmax_pv(s, v, mult):
    if mult is not None:
        s = jnp.where(mult > 0, s, NEG)
    m = jnp.max(s, axis=-1, keepdims=True)
    p = jnp.exp(s - m)
    if mult is not None:
        p = p * mult.astype(F32)
    l = jnp.sum(p, axis=-1, keepdims=True)
    return _dot(p.astype(BF16), v), l


def _attn_a_kernel(q_ref, k_ref, v_ref, m_ref, o_ref):
    q = q_ref[...]
    k = k_ref[...]
    v = v_ref[...]
    mult = m_ref[...]
    lane = lax.broadcasted_iota(jnp.int32, (1, W_ATT), 1)
    acc = jnp.zeros(q.shape, F32)
    for h in range(N_HEADS):
        hm = (lane >= h * HEAD_DIM) & (lane < (h + 1) * HEAD_DIM)
        qh = jnp.where(hm, q, jnp.zeros_like(q))
        pv, l = _softmax_pv(_dot_nt(qh, k), v, mult)
        acc = acc + jnp.where(hm, pv * (1.0 / l), 0.0)
    o_ref[...] = acc.astype(o_ref.dtype)


def _attn_a(q, k, v, mult, batch, seq, tq=256):
    nq = seq // tq
    return pl.pallas_call(
        _attn_a_kernel,
        grid=(nq, batch),
        in_specs=[pl.BlockSpec((tq, W_ATT), lambda i, b: (b * nq + i, 0)),
                  pl.BlockSpec((seq, W_ATT), lambda i, b: (b, 0)),
                  pl.BlockSpec((seq, W_ATT), lambda i, b: (b, 0)),
                  pl.BlockSpec((tq, seq), lambda i, b: (i, 0))],
        out_specs=pl.BlockSpec((tq, W_ATT), lambda i, b: (b * nq + i, 0)),
        out_shape=jax.ShapeDtypeStruct((batch * seq, W_ATT), BF16),
        compiler_params=_cparams(("parallel", "parallel")),
        name="attn_a",
    )(q, k, v, mult)


def _attn_c_kernel(q_ref, k_ref, v_ref, o_ref):
    k = k_ref[...]
    v = v_ref[...]
    lane = lax.broadcasted_iota(jnp.int32, (1, 128), 1)
    for pair in range(2):
        q = q_ref[:, pair * 128:(pair + 1) * 128]
        acc = jnp.zeros(q.shape, F32)
        for half in range(2):
            hm = (lane >= half * HEAD_DIM) & (lane < (half + 1) * HEAD_DIM)
            qh = jnp.where(hm, q, jnp.zeros_like(q))
            pv, l = _softmax_pv(_dot_nt(qh, k), v, None)
            acc = acc + jnp.where(hm, pv * (1.0 / l), 0.0)
        o_ref[:, pair * 128:(pair + 1) * 128] = acc.astype(o_ref.dtype)


def _attn_c(q, k, v, batch, seq, tq=256):
    nq = seq // tq
    return pl.pallas_call(
        _attn_c_kernel,
        grid=(batch, nq),
        in_specs=[pl.BlockSpec((tq, W_ATT), lambda b, i: (b * nq + i, 0)),
                  pl.BlockSpec((seq, 128), lambda b, i: (b, 0)),
                  pl.BlockSpec((seq, 128), lambda b, i: (b, 0))],
        out_specs=pl.BlockSpec((tq, W_ATT), lambda b, i: (b * nq + i, 0)),
        out_shape=jax.ShapeDtypeStruct((batch * seq, W_ATT), BF16),
        compiler_params=_cparams(("parallel", "parallel")),
        name="attn_c",
    )(q, k, v)


def _gdn_prep_kernel(x_ref, w_ref, o_ref):
    j = pl.program_id(1)
    x = x_ref[...]
    w = w_ref[...]
    seq = x.shape[0]
    pos = lax.broadcasted_iota(jnp.int32, (seq, 1), 0)
    acc = x * w[B_CONV // 2:B_CONV // 2 + 1, :]
    for tap in range(B_CONV):
        off = tap - B_CONV // 2
        if off == 0:
            continue
        sh = pltpu.roll(x, (-off) % seq, 0)
        ok = (pos + off >= 0) & (pos + off < seq)
        acc = acc + jnp.where(ok, sh, 0.0) * w[tap:tap + 1, :]
    y = acc * _sigmoid(acc)
    norm_scale = jnp.where(j == 0, B_DK ** -0.5, 1.0).astype(F32)
    for h in range(N_HEADS):
        yh = y[:, h * B_DK:(h + 1) * B_DK]
        ss = jnp.sum(yh * yh, axis=-1, keepdims=True)
        nh = yh * (lax.rsqrt(ss + EPS) * norm_scale)
        o_ref[:, h * B_DK:(h + 1) * B_DK] = jnp.where(j < 2, nh, yh).astype(o_ref.dtype)


def _gdn_prep(proj, conv_w, batch, seq):
    return pl.pallas_call(
        _gdn_prep_kernel,
        grid=(batch, 3),
        in_specs=[pl.BlockSpec((seq, W_GDN), lambda b, j: (b, j)),
                  pl.BlockSpec((B_CONV, W_GDN), lambda b, j: (0, j))],
        out_specs=pl.BlockSpec((seq, W_GDN), lambda b, j: (b, j)),
        out_shape=jax.ShapeDtypeStruct((batch * seq, 3 * W_GDN), BF16),
        compiler_params=_cparams(("parallel", "parallel")),
        name="gdn_prep",
    )(proj, conv_w)


def _lane_block(width, block, h):
    lane = lax.broadcasted_iota(jnp.int32, (1, width), 1)
    return (lane >= h * block) & (lane < (h + 1) * block)


def _block_diag(x, block):
    width = x.shape[1]
    zero = jnp.zeros_like(x)
    return jnp.concatenate([jnp.where(_lane_block(width, block, h), x, zero) for h in range(N_HEADS)], axis=0)


def _unit_tri_inverse_cat(a_list, eye_t):
    p = [eye_t - a for a in a_list]
    ab = [a.astype(BF16) for a in a_list]
    ap = [_dot(x, _block_diag(x, CHUNK)) for x in ab]
    for lvl in range(5):
        apb = [x.astype(BF16) for x in ap]
        bd = [_block_diag(x, CHUNK) for x in apb]
        if lvl < 4:
            r = [_dot(jnp.concatenate([pi.astype(BF16), xi], axis=0), bi) for pi, xi, bi in zip(p, apb, bd)]
            p = [pi + ri[0:CHUNK] for pi, ri in zip(p, r)]
            ap = [ri[CHUNK:2 * CHUNK] for ri in r]
        else:
            p = [pi + _dot(pi.astype(BF16), bi) for pi, bi in zip(p, bd)]
    return p


def _chunk_cumsum(g, rev):
    ridx = lax.broadcasted_iota(jnp.int32, g.shape, 0)
    x = g
    s = 1
    while s < CHUNK:
        if rev:
            sh = jnp.where(ridx < CHUNK - s, pltpu.roll(x, CHUNK - s, 0), 0.0)
        else:
            sh = jnp.where(ridx >= s, pltpu.roll(x, s, 0), 0.0)
        x = x + sh
        s *= 2
    return x


def _gdn_chunk_operands(qkv_ref, ab_ref, alog, dtb, rows, rev):
    c0 = N_HEADS if rev else 0
    ab = ab_ref[rows, :]
    xg = ab + dtb
    softplus = jnp.maximum(xg, 0.0) + jnp.log(1.0 + jnp.exp(-jnp.abs(xg)))
    g = -jnp.exp(alog) * softplus
    beta = _sigmoid(ab)
    gc = _chunk_cumsum(g, rev)
    g_last = gc[0:1, :] if rev else gc[CHUNK - 1:CHUNK, :]
    eg = jnp.exp(gc)
    e_rest = jnp.exp(g_last - gc)

    wcat = N_HEADS * CHUNK
    ii = lax.broadcasted_iota(jnp.int32, (CHUNK, 1), 0)
    jl = lax.broadcasted_iota(jnp.int32, (1, wcat), 1) & (CHUNK - 1)
    eye_b = ii == jl
    incl = (ii <= jl) if rev else (ii >= jl)
    g_col = jnp.zeros((CHUNK, wcat), F32)
    for h in range(N_HEADS):
        g_col = jnp.where(_lane_block(wcat, CHUNK, h), gc[:, c0 + h:c0 + h + 1], g_col)
    g_row = jnp.sum(jnp.where(eye_b, g_col, 0.0), axis=0, keepdims=True)
    dec = jnp.exp(jnp.where(incl, g_col - g_row, NEG))

    kb, vb, kbg, qg, kd = [], [], [], [], []
    for h in range(N_HEADS):
        c = c0 + h
        qh = qkv_ref[rows, h * B_DK:(h + 1) * B_DK].astype(F32)
        kh = qkv_ref[rows, W_GDN + h * B_DK:W_GDN + (h + 1) * B_DK].astype(F32)
        vh = qkv_ref[rows, 2 * W_GDN + h * B_DK:2 * W_GDN + (h + 1) * B_DK].astype(F32)
        bcol = beta[:, 8 + c:9 + c]
        egc = eg[:, c:c + 1]
        kbh = kh * bcol
        kb.append(kbh)
        vb.append(vh * bcol)
        kbg.append(kbh * egc)
        qg.append(qh * egc)
        kd.append((kh * e_rest[:, c:c + 1]).astype(BF16))
    cat = lambda xs: jnp.concatenate(xs, axis=1)
    return dict(dec=dec, eye_b=eye_b, kb=cat(kb).astype(BF16), vb=cat(vb).astype(BF16), kbg=cat(kbg).astype(BF16),
                qg=cat(qg), kd=kd, eg_last=jnp.exp(g_last),
                k=qkv_ref[rows, W_GDN:2 * W_GDN], q=qkv_ref[rows, 0:W_GDN])


def _gdn_solve_chunks(ops):
    kq = [_dot_nt(jnp.concatenate([o["kb"], o["q"]], axis=0), _block_diag(o["k"], B_DK)) for o in ops]
    a_cat = [jnp.where(o["eye_b"], 0.0, x[0:CHUNK] * o["dec"]) for o, x in zip(ops, kq)]
    qk = [(x[CHUNK:2 * CHUNK] * o["dec"]).astype(BF16) for o, x in zip(ops, kq)]
    eye_t = jnp.where(ops[0]["eye_b"], 1.0, 0.0).astype(F32)
    t_cat = [t.astype(BF16) for t in _unit_tri_inverse_cat(a_cat, eye_t)]
    ub = [_dot(t, _block_diag(o["vb"], B_DK)).astype(BF16) for o, t in zip(ops, t_cat)]
    wb = [_dot(t, _block_diag(o["kbg"], B_DK)).astype(BF16) for o, t in zip(ops, t_cat)]
    q_eff = [o["qg"] - _dot(x, _block_diag(w, B_DK)) for o, x, w in zip(ops, qk, wb)]
    o_intra = [_dot(x, _block_diag(u, B_DK)) for x, u in zip(qk, ub)]
    mc = []
    for o, u, w in zip(ops, ub, wb):
        per_head = []
        for h in range(N_HEADS):
            cols = slice(h * B_DK, (h + 1) * B_DK)
            per_head.append(_dot_tn(o["kd"][h], jnp.concatenate([w[:, cols], u[:, cols]], axis=1)))
        mc.append(per_head)
    return mc, q_eff, o_intra


def _gdn_kernel(qkv_ref, ab_ref, alog_ref, dtb_ref, o_ref, s_ref, m_ref, c_ref, qe_ref, egl_ref, *, rev, group):
    seq = qkv_ref.shape[0]
    nc = seq // CHUNK
    c0 = N_HEADS if rev else 0
    alog = alog_ref[...]
    dtb = dtb_ref[...]

    def solve(i, carry):
        ns = [i * group + j for j in range(group)]
        rows = [pl.ds(pl.multiple_of(n * CHUNK, CHUNK), CHUNK) for n in ns]
        ops = [_gdn_chunk_operands(qkv_ref, ab_ref, alog, dtb, r, rev) for r in rows]
        mc, q_eff, o_intra = _gdn_solve_chunks(ops)
        for j, n in enumerate(ns):
            for h in range(N_HEADS):
                m_ref[n * N_HEADS + h] = mc[j][h][:, 0:B_DK].astype(BF16)
                c_ref[n * N_HEADS + h] = mc[j][h][:, B_DK:2 * B_DK]
            qe_ref[rows[j], :] = q_eff[j].astype(BF16)
            o_ref[rows[j], :] = o_intra[j]
            egl_ref[n] = jnp.broadcast_to(ops[j]["eg_last"], (8, 128))
        return carry

    lax.fori_loop(0, nc // group, solve, 0)

    s_ref[...] = jnp.zeros(s_ref.shape, F32)

    def scan(i, carry):
        n = (nc - 1 - i) if rev else i
        rows = pl.ds(pl.multiple_of(n * CHUNK, CHUNK), CHUNK)
        egl = egl_ref[n]
        new_s, new_o = [], []
        for h in range(N_HEADS):
            cols = slice(h * B_DK, (h + 1) * B_DK)
            state = s_ref[h]
            lhs = jnp.concatenate([m_ref[n * N_HEADS + h], qe_ref[rows, cols]], axis=0)
            r = _dot(lhs, state.astype(BF16))
            new_s.append(state * egl[0:1, c0 + h:c0 + h + 1] - r[0:B_DK] + c_ref[n * N_HEADS + h])
            new_o.append(o_ref[rows, cols] + r[B_DK:B_DK + CHUNK])
        for h in range(N_HEADS):
            s_ref[h] = new_s[h]
            o_ref[rows, h * B_DK:(h + 1) * B_DK] = new_o[h]
        return carry

    lax.fori_loop(0, nc, scan, 0)


def _gdn(qkv, proj, alog, dtb, batch, seq, rev, group=8):
    nc = seq // CHUNK
    return pl.pallas_call(
        functools.partial(_gdn_kernel, rev=rev, group=group),
        grid=(batch,),
        in_specs=[pl.BlockSpec((seq, 3 * W_GDN), lambda b: (b, 0)),
                  pl.BlockSpec((seq, 128), lambda b: (b, COL_AB // 128)),
                  pl.BlockSpec((1, 128), lambda b: (0, 0)),
                  pl.BlockSpec((1, 128), lambda b: (0, 0))],
        out_specs=pl.BlockSpec((seq, W_GDN), lambda b: (b, 0)),
        out_shape=jax.ShapeDtypeStruct((batch * seq, W_GDN), F32),
        scratch_shapes=[pltpu.VMEM((N_HEADS, B_DK, B_DK), F32),
                        pltpu.VMEM((nc * N_HEADS, B_DK, B_DK), BF16),
                        pltpu.VMEM((nc * N_HEADS, B_DK, B_DK), F32),
                        pltpu.VMEM((seq, W_GDN), BF16),
                        pltpu.VMEM((nc, 8, 128), F32)],
        compiler_params=_cparams(("parallel",)),
        name="gdn_bwd" if rev else "gdn_fwd",
    )(qkv, proj, alog, dtb)


def _outproj_kernel(x_ref, oa_ref, of_ref, ob_ref, z_ref, oc_ref, onorm_ref, wa_ref, wb_ref, wc_ref, o_ref):
    acc = x_ref[...] + _dot(oa_ref[...], wa_ref[...])
    acc = acc + _dot(oc_ref[...], wc_ref[...])
    for h in range(N_HEADS):
        cols = slice(h * B_DK, (h + 1) * B_DK)
        o = of_ref[:, cols] + ob_ref[:, cols]
        ms = jnp.mean(o * o, axis=-1, keepdims=True)
        y = (o * lax.rsqrt(ms + EPS)) * onorm_ref[...]
        z = z_ref[:, cols]
        gated = (y * (z * _sigmoid(z))).astype(BF16)
        acc = acc + _dot(gated, wb_ref[h * B_DK:(h + 1) * B_DK, :])
    o_ref[...] = acc


def _out_proj(x, oa, o_f, o_b, proj, oc, onorm, wa, wb, wc, tm=512):
    t, d = x.shape
    row = lambda w: pl.BlockSpec((tm, w), lambda i: (i, 0))
    full = lambda w: pl.BlockSpec((w, d), lambda i: (0, 0))
    return pl.pallas_call(
        _outproj_kernel,
        grid=(t // tm,),
        in_specs=[row(d), row(W_ATT), row(W_GDN), row(W_GDN),
                  pl.BlockSpec((tm, W_GDN), lambda i: (i, COL_BZ // W_GDN)),
                  row(W_ATT), pl.BlockSpec((1, B_DK), lambda i: (0, 0)),
                  full(W_ATT), full(W_GDN), full(W_ATT)],
        out_specs=row(d),
        out_shape=jax.ShapeDtypeStruct((t, d), F32),
        compiler_params=_cparams(("parallel",)),
        name="out_proj",
    )(x, oa, o_f, o_b, proj, oc, onorm, wa, wb, wc)


def _ffn_kernel(x_ref, nw_ref, wg_ref, wu_ref, wd_ref, o_ref, xn_ref, acc_ref):
    j = pl.program_id(1)

    @pl.when(j == 0)
    def _():
        x = x_ref[...]
        ms = jnp.mean(x * x, axis=-1, keepdims=True)
        xn_ref[...] = ((x * lax.rsqrt(ms + EPS)) * nw_ref[...]).astype(BF16)
        acc_ref[...] = x

    xn = xn_ref[...]
    gate = _dot(xn, wg_ref[...])
    up = _dot(xn, wu_ref[...])
    hidden = (gate * _sigmoid(gate)) * up
    acc_ref[...] += _dot(hidden.astype(BF16), wd_ref[...])

    @pl.when(j == pl.num_programs(1) - 1)
    def _():
        o_ref[...] = acc_ref[...]


def _ffn(x, nw, w_gate_up, w_down, tm=512, tf=1408):
    t, d = x.shape
    dff = w_down.shape[0]
    nf = dff // tf
    return pl.pallas_call(
        _ffn_kernel,
        grid=(t // tm, nf),
        in_specs=[pl.BlockSpec((tm, d), lambda i, j: (i, 0)),
                  pl.BlockSpec((1, d), lambda i, j: (0, 0)),
                  pl.BlockSpec((d, tf), lambda i, j: (0, j)),
                  pl.BlockSpec((d, tf), lambda i, j: (0, j + nf)),
                  pl.BlockSpec((tf, d), lambda i, j: (j, 0))],
        out_specs=pl.BlockSpec((tm, d), lambda i, j: (i, 0)),
        out_shape=jax.ShapeDtypeStruct((t, d), F32),
        scratch_shapes=[pltpu.VMEM((tm, d), BF16), pltpu.VMEM((tm, d), F32)],
        compiler_params=_cparams(("parallel", "arbitrary")),
        name="ffn",
    )(x, nw, w_gate_up, w_gate_up, w_down)


def _rope_tables(seq):
    t = np.arange(seq, dtype=np.float64)

    def build(groups):
        cos = np.ones((seq, HEAD_DIM))
        s_lo = np.zeros((seq, HEAD_DIM))
        s_hi = np.zeros((seq, HEAD_DIM))
        for start, half, theta, pos in groups:
            inv = np.float64(np.float32(theta)) ** (-np.arange(half, dtype=np.float64) / half)
            ang = pos[:, None] * inv[None, :]
            cos[:, start:start + half] = np.cos(ang)
            cos[:, start + half:start + 2 * half] = np.cos(ang)
            s_lo[:, start:start + half] = -np.sin(ang)
            s_hi[:, start + half:start + 2 * half] = np.sin(ang)
        tab = np.stack([cos, s_lo, s_hi])
        return np.tile(tab, (1, 1, N_HEADS)).astype(np.float32)

    tab_a = build([(0, A_ROT_HALF, A_THETA, t)])
    row = np.floor(t / GRID_W)
    col = t - row * GRID_W
    tab_c = build([(0, C_ROT_HALF, C_THETA, row), (2 * C_ROT_HALF, C_ROT_HALF, C_THETA, col)])
    return tab_a, tab_c


def _multiplicity(seq):
    d = np.arange(seq)[:, None] - np.arange(seq)[None, :]
    m = np.zeros((seq, seq), np.float32)
    for window, dil in A_PATTERNS:
        radius = window // (2 * dil)
        m += ((d % dil == 0) & (np.abs(d) <= radius * dil)).astype(np.float32)
    return m


def _head_mean_matrix():
    e = np.kron(np.eye(N_HEADS), np.full((HEAD_DIM, HEAD_DIM), 1.0 / HEAD_DIM))
    return e.astype(np.float32)


def _pad_lanes(v, width):
    return jnp.pad(v.reshape(1, -1), ((0, 0), (0, width - v.size)))


def kernel(x, norm1, w_in, qn_a, kn_a, conv_b, a_log_b, dt_bias_b, onorm_b, qn_c, kn_c, w_out, norm2, w_gate_up, w_down):
    batch, seq, d = x.shape
    depth = w_in.shape[0]
    t = batch * seq

    tab_a, tab_c = _rope_tables(seq)
    tab_a, tab_c = jnp.asarray(tab_a), jnp.asarray(tab_c)
    mult = jnp.asarray(_multiplicity(seq), dtype=BF16)
    e_mat = jnp.asarray(_head_mean_matrix(), dtype=BF16)

    c0 = 3 * W_ATT
    c1 = c0 + 4 * W_GDN
    c2 = c1 + 4 * N_HEADS
    cq = [w_in[:, :, c2 + h * HEAD_DIM:c2 + (h + 1) * HEAD_DIM] for h in C_HEAD_ORDER]
    w_in_p = jnp.concatenate(
        [w_in[:, :, c0:c1], w_in[:, :, 0:c0]] + cq + [w_in[:, :, c2 + W_ATT:], w_in[:, :, c1:c2]], axis=-1)
    w_in_p = jnp.pad(w_in_p, ((0, 0), (0, 0), (0, IN_PAD - w_in_p.shape[-1]))).astype(BF16)

    w_out_a = w_out[:, 0:W_ATT].astype(BF16)
    w_out_b = w_out[:, W_ATT:W_ATT + W_GDN].astype(BF16)
    wc = w_out[:, W_ATT + W_GDN:]
    w_out_c = jnp.concatenate([wc[:, h * HEAD_DIM:(h + 1) * HEAD_DIM] for h in C_HEAD_ORDER], axis=1).astype(BF16)
    w_gu = w_gate_up.astype(BF16)
    w_dn = w_down.astype(BF16)

    tile4 = lambda v: jnp.tile(v.reshape(1, -1), (1, N_HEADS))

    xt = x.reshape(t, d)
    for i in range(depth):
        proj = _in_proj(xt, norm1[i].reshape(1, d), w_in_p[i])
        qa, ka, va, qc, kc, vc = _attn_prep(proj, tile4(qn_a[i]), tile4(kn_a[i]), tile4(qn_c[i]), tile4(kn_c[i]),
                                            tab_a, tab_c, e_mat, seq)
        o_a = _attn_a(qa, ka, va, mult, batch, seq)
        o_c = _attn_c(qc, kc, vc, batch, seq)
        qkv_b = _gdn_prep(proj, conv_b[i], batch, seq)
        alog, dtb = _pad_lanes(a_log_b[i], 128), _pad_lanes(dt_bias_b[i], 128)
        o_f = _gdn(qkv_b, proj, alog, dtb, batch, seq, rev=False)
        o_r = _gdn(qkv_b, proj, alog, dtb, batch, seq, rev=True)
        xt = _out_proj(xt, o_a, o_f, o_r, proj, o_c, onorm_b[i].reshape(1, B_DK), w_out_a[i], w_out_b[i], w_out_c[i])
        xt = _ffn(xt, norm2[i].reshape(1, d), w_gu[i], w_dn[i])
    return xt.reshape(batch, seq, d)
```

```python
import functools

import numpy as np
import jax
import jax.numpy as jnp
from jax import lax
from jax.experimental import pallas as pl
from jax.experimental.pallas import tpu as pltpu

F32 = jnp.float32
BF16 = jnp.bfloat16

EPS = 1e-6
GRID_W = 64
HEAD_DIM = 64
N_HEADS = 4
A_PATTERNS = ((128, 1), (512, 4), (2048, 16))
A_ROT_HALF = 8
A_THETA = 500000.0
C_ROT_HALF = 16
C_THETA = 10000.0
B_DK = 128
B_CONV = 5
CHUNK = 64
NEG = -1e30

W_ATT = N_HEADS * HEAD_DIM
W_GDN = N_HEADS * B_DK
COL_BQ, COL_BK, COL_BV, COL_BZ = 0, 512, 1024, 1536
COL_AQ, COL_AK, COL_AV = 2048, 2304, 2560
COL_CQ, COL_CK, COL_CV = 2816, 3072, 3200
COL_AB = 3328
IN_PAD = 3456
N_CLS = 4
A_LOCAL_TQ = 128
C_HEAD_ORDER = (0, 2, 1, 3)

VMEM_LIMIT = 56 * 1024 * 1024


def _cparams(sem):
    return pltpu.CompilerParams(dimension_semantics=sem, vmem_limit_bytes=VMEM_LIMIT)


def _split2(x):
    hi = x.astype(BF16)
    lo = (x - hi.astype(F32)).astype(BF16)
    return hi, lo


def _dot(a, b):
    return jnp.dot(a, b, preferred_element_type=F32)


def _dot_nt(a, b):
    return lax.dot_general(a, b, (((1,), (1,)), ((), ())), preferred_element_type=F32)


def _dot_tn(a, b):
    return lax.dot_general(a, b, (((0,), (0,)), ((), ())), preferred_element_type=F32)


def _sigmoid(x):
    return 1.0 / (1.0 + jnp.exp(-x))


def _inproj_kernel(x_ref, nw_ref, w_ref, o_ref, oab_ref):
    x = x_ref[...]
    ms = jnp.mean(x * x, axis=-1, keepdims=True)
    y = (x * lax.rsqrt(ms + EPS)) * nw_ref[...]
    res = _dot(y.astype(BF16), w_ref[...])
    o_ref[...] = res[:, 0:COL_AB].astype(BF16)
    oab_ref[...] = res[:, COL_AB:IN_PAD]


def _in_proj(x, nw, w, tm=256):
    t, d = x.shape
    n = w.shape[1]
    return pl.pallas_call(
        _inproj_kernel,
        grid=(t // tm,),
        in_specs=[pl.BlockSpec((tm, d), lambda i: (i, 0)),
                  pl.BlockSpec((1, d), lambda i: (0, 0)),
                  pl.BlockSpec((d, n), lambda i: (0, 0))],
        out_specs=[pl.BlockSpec((tm, COL_AB), lambda i: (i, 0)),
                   pl.BlockSpec((tm, n - COL_AB), lambda i: (i, 0))],
        out_shape=[jax.ShapeDtypeStruct((t, COL_AB), BF16), jax.ShapeDtypeStruct((t, n - COL_AB), F32)],
        compiler_params=_cparams(("parallel",)),
        name="in_proj",
    )(x, nw, w)


def _head_norm_rope(x, w, e_hi, cos, sin_lo, sin_hi, rot_half, scale):
    xx = x * x
    hi, lo = _split2(xx)
    ms = _dot(hi, e_hi) + _dot(lo, e_hi)
    y = (x * lax.rsqrt(ms + EPS)) * w
    width = x.shape[1]
    up = pltpu.roll(y, width - rot_half, 1)
    dn = pltpu.roll(y, rot_half, 1)
    out = y * cos + up * sin_lo + dn * sin_hi
    return out * scale


def _attn_prep_kernel(aq_ref, ak_ref, av_ref, cq_ref, ck_ref, cv_ref,
                      qna_ref, kna_ref, qnc_ref, knc_ref, ta_ref, tc_ref, e_ref,
                      oqa_cls_ref, oka_cls_ref, ova_cls_ref, oqa_ref, oka_ref, ova_ref, oqc_ref, okc_ref, ovc_ref):
    f32 = lambda r: r[...].astype(F32)
    e = e_ref[...]
    cos_a, sl_a, sh_a = ta_ref[0], ta_ref[1], ta_ref[2]
    cos_c, sl_c, sh_c = tc_ref[0], tc_ref[1], tc_ref[2]
    scale = HEAD_DIM ** -0.5
    qa = _head_norm_rope(f32(aq_ref), qna_ref[...], e, cos_a, sl_a, sh_a, A_ROT_HALF, scale).astype(BF16)
    ka = _head_norm_rope(f32(ak_ref), kna_ref[...], e, cos_a, sl_a, sh_a, A_ROT_HALF, 1.0).astype(BF16)
    va = av_ref[...]
    oqa_cls_ref[...] = qa
    oka_cls_ref[...] = ka
    ova_cls_ref[...] = va
    oqa_ref[...] = qa
    oka_ref[...] = ka
    ova_ref[...] = va
    oqc_ref[...] = _head_norm_rope(f32(cq_ref), qnc_ref[...], e, cos_c, sl_c, sh_c, C_ROT_HALF, scale).astype(BF16)
    okc_ref[...] = _head_norm_rope(f32(ck_ref), knc_ref[...][:, :128], e[:128, :128],
                                   cos_c[:, :128], sl_c[:, :128], sh_c[:, :128], C_ROT_HALF, 1.0).astype(BF16)
    ovc_ref[...] = cv_ref[...]


def _attn_prep(proj, qna, kna, qnc, knc, tab_a, tab_c, e_mat, batch, seq):
    t, width = proj.shape
    rows = seq // N_CLS
    proj_v = proj.reshape(t // N_CLS, N_CLS * width)
    cb = lambda col, w: pl.BlockSpec((rows, w), lambda g, c0=col // w, n=width // w: (g // N_CLS, (g % N_CLS) * n + c0))
    vec = pl.BlockSpec((1, W_ATT), lambda g: (0, 0))
    tab = pl.BlockSpec((3, rows, W_ATT), lambda g: (0, g % N_CLS, 0))
    cls_out = lambda w: pl.BlockSpec((rows, w), lambda g: (g, 0))
    tok_out = lambda w: pl.BlockSpec((rows, w), lambda g: (g // N_CLS, g % N_CLS))
    widths = (256, 256, 256, 256, 128, 128)
    outs = pl.pallas_call(
        _attn_prep_kernel,
        grid=(batch * N_CLS,),
        in_specs=[cb(COL_AQ, 256), cb(COL_AK, 256), cb(COL_AV, 256),
                  cb(COL_CQ, 256), cb(COL_CK, 128), cb(COL_CV, 128),
                  vec, vec, vec, vec, tab, tab,
                  pl.BlockSpec((W_ATT, W_ATT), lambda g: (0, 0))],
        out_specs=[cls_out(256), cls_out(256), cls_out(256)] + [tok_out(w) for w in widths],
        out_shape=[jax.ShapeDtypeStruct((t, 256), BF16)] * 3
                  + [jax.ShapeDtypeStruct((t // N_CLS, N_CLS * w), BF16) for w in widths],
        compiler_params=_cparams(("parallel",)),
        name="attn_prep",
    )(proj_v, proj_v, proj_v, proj_v, proj_v, proj_v, qna, kna, qnc, knc, tab_a, tab_c, e_mat)
    cls = outs[:3]
    tok = [o.reshape(t, w) for o, w in zip(outs[3:], widths)]
    return cls, tok


def _head_masks(width):
    lane = lax.broadcasted_iota(jnp.int32, (1, width), 1)
    return [(lane >= h * HEAD_DIM) & (lane < (h + 1) * HEAD_DIM) for h in range(width // HEAD_DIM)]


def _attn_a_local_kernel(q_ref, k_ref, v_ref, m_ref, o1_ref, lse_ref):
    i = pl.program_id(1)
    seq = k_ref.shape[0]
    tq, win = m_ref.shape[1], m_ref.shape[2]
    nblk = q_ref.shape[0] // tq
    last = seq // tq - 1
    hms = _head_masks(W_ATT)
    vs, scores = [], []
    for j in range(nblk):
        blk = i * nblk + j
        start = jnp.clip(blk * tq - (win - tq) // 2, 0, seq - win)
        rows = pl.ds(pl.multiple_of(start, 64), win)
        k = k_ref[rows, :]
        vs.append(v_ref[rows, :])
        mult = m_ref[1]
        if j == 0:
            mult = jnp.where(blk == 0, m_ref[0], mult)
        if j == nblk - 1:
            mult = jnp.where(blk == last, m_ref[2], mult)
        q = q_ref[j * tq:(j + 1) * tq, :]
        scores.append([jnp.where(mult > 0, _dot_nt(jnp.where(hm, q, jnp.zeros_like(q)), k), NEG) for hm in hms])
    maxes = [[jnp.max(s, axis=-1, keepdims=True) for s in row] for row in scores]
    probs = [[jnp.exp(s - m) for s, m in zip(srow, mrow)] for srow, mrow in zip(scores, maxes)]
    sums = [[jnp.sum(p, axis=-1, keepdims=True) for p in row] for row in probs]
    pvs = [[_dot(p.astype(BF16), v) for p in row] for row, v in zip(probs, vs)]
    for j in range(nblk):
        out = jnp.zeros((tq, W_ATT), F32)
        lse = jnp.zeros((tq, W_ATT), F32)
        for h in range(N_HEADS):
            out = jnp.where(hms[h], pvs[j][h] * (1.0 / sums[j][h]), out)
            lse = jnp.where(hms[h], maxes[j][h] + jnp.log(sums[j][h]), lse)
        o1_ref[j * tq:(j + 1) * tq, :] = out
        lse_ref[j * tq:(j + 1) * tq, :] = lse


def _attn_a_local(q, k, v, mult, batch, seq, nblk=4):
    tq = mult.shape[1]
    rows = nblk * tq
    nq = seq // rows
    return pl.pallas_call(
        _attn_a_local_kernel,
        grid=(batch, nq),
        in_specs=[pl.BlockSpec((rows, W_ATT), lambda b, i: (b * nq + i, 0)),
                  pl.BlockSpec((seq, W_ATT), lambda b, i: (b, 0)),
                  pl.BlockSpec((seq, W_ATT), lambda b, i: (b, 0)),
                  pl.BlockSpec(mult.shape, lambda b, i: (0, 0, 0))],
        out_specs=[pl.BlockSpec((rows, W_ATT), lambda b, i: (b * nq + i, 0)),
                   pl.BlockSpec((rows, W_ATT), lambda b, i: (b * nq + i, 0))],
        out_shape=[jax.ShapeDtypeStruct((batch * seq, W_ATT), F32), jax.ShapeDtypeStruct((batch * seq, W_ATT), F32)],
        compiler_params=_cparams(("parallel", "parallel")),
        name="attn_a_local",
    )(q, k, v, mult)


def _attn_a_dilated_kernel(q_ref, k_ref, v_ref, m_ref, o1_ref, lse1_ref, o_ref):
    q = q_ref[...]
    k = k_ref[...]
    v = v_ref[...]
    mult = m_ref[...]
    multf = mult.astype(F32)
    hms = _head_masks(W_ATT)
    scores = [jnp.where(mult > 0, _dot_nt(jnp.where(hm, q, jnp.zeros_like(q)), k), NEG) for hm in hms]
    m2 = [jnp.max(s, axis=-1, keepdims=True) for s in scores]
    probs = [jnp.exp(s - m) * multf for s, m in zip(scores, m2)]
    l2 = [jnp.sum(p, axis=-1, keepdims=True) for p in probs]
    pv2 = [_dot(p.astype(BF16), v) for p in probs]
    m2b = jnp.zeros(q.shape, F32)
    l2b = jnp.zeros(q.shape, F32)
    pv2b = jnp.zeros(q.shape, F32)
    for h in range(N_HEADS):
        m2b = jnp.where(hms[h], m2[h], m2b)
        l2b = jnp.where(hms[h], l2[h], l2b)
        pv2b = jnp.where(hms[h], pv2[h], pv2b)
    lse1 = lse1_ref[...]
    m = jnp.maximum(lse1, m2b)
    a1 = jnp.exp(lse1 - m)
    a2 = jnp.exp(m2b - m)
    o_ref[...] = ((o1_ref[...] * a1 + pv2b * a2) / (a1 + l2b * a2)).astype(o_ref.dtype)


def _attn_a_dilated(q, k, v, mult, o1, lse1, batch, seq, tq=256):
    t = batch * seq
    rows = seq // N_CLS
    nq = rows // tq
    o1_v = o1.reshape(t // N_CLS, N_CLS * W_ATT)
    lse1_v = lse1.reshape(t // N_CLS, N_CLS * W_ATT)
    tok = lambda w: pl.BlockSpec((tq, w), lambda i, g: ((g // N_CLS) * nq + i, g % N_CLS))
    out = pl.pallas_call(
        _attn_a_dilated_kernel,
        grid=(nq, batch * N_CLS),
        in_specs=[pl.BlockSpec((tq, W_ATT), lambda i, g: (g * nq + i, 0)),
                  pl.BlockSpec((rows, W_ATT), lambda i, g: (g, 0)),
                  pl.BlockSpec((rows, W_ATT), lambda i, g: (g, 0)),
                  pl.BlockSpec((tq, rows), lambda i, g: (i, 0)),
                  tok(W_ATT), tok(W_ATT)],
        out_specs=tok(W_ATT),
        out_shape=jax.ShapeDtypeStruct((t // N_CLS, N_CLS * W_ATT), BF16),
        compiler_params=_cparams(("parallel", "parallel")),
        name="attn_a_dilated",
    )(q, k, v, mult, o1_v, lse1_v)
    return out.reshape(t, W_ATT)


def _attn_c_kernel(q_ref, k_ref, v_ref, o_ref):
    k = k_ref[...]
    v = v_ref[...]
    hms = _head_masks(128)
    qs = [q_ref[:, pair * 128:(pair + 1) * 128] for pair in range(2)]
    scores = [[_dot_nt(jnp.where(hm, q, jnp.zeros_like(q)), k) for hm in hms] for q in qs]
    maxes = [[jnp.max(s, axis=-1, keepdims=True) for s in row] for row in scores]
    probs = [[jnp.exp(s - m) for s, m in zip(srow, mrow)] for srow, mrow in zip(scores, maxes)]
    sums = [[jnp.sum(p, axis=-1, keepdims=True) for p in row] for row in probs]
    pvs = [[_dot(p.astype(BF16), v) for p in row] for row in probs]
    for pair in range(2):
        acc = jnp.zeros(qs[pair].shape, F32)
        for half in range(2):
            acc = jnp.where(hms[half], pvs[pair][half] * (1.0 / sums[pair][half]), acc)
        o_ref[:, pair * 128:(pair + 1) * 128] = acc.astype(o_ref.dtype)


def _attn_c(q, k, v, batch, seq, tq=256):
    nq = seq // tq
    return pl.pallas_call(
        _attn_c_kernel,
        grid=(batch, nq),
        in_specs=[pl.BlockSpec((tq, W_ATT), lambda b, i: (b * nq + i, 0)),
                  pl.BlockSpec((seq, 128), lambda b, i: (b, 0)),
                  pl.BlockSpec((seq, 128), lambda b, i: (b, 0))],
        out_specs=pl.BlockSpec((tq, W_ATT), lambda b, i: (b * nq + i, 0)),
        out_shape=jax.ShapeDtypeStruct((batch * seq, W_ATT), BF16),
        compiler_params=_cparams(("parallel", "parallel")),
        name="attn_c",
    )(q, k, v)


def _gdn_prep_kernel(x_ref, w_ref, o_ref):
    j = pl.program_id(1)
    x = x_ref[...].astype(F32)
    w = w_ref[...]
    seq = x.shape[0]
    pos = lax.broadcasted_iota(jnp.int32, (seq, 1), 0)
    acc = x * w[B_CONV // 2:B_CONV // 2 + 1, :]
    for tap in range(B_CONV):
        off = tap - B_CONV // 2
        if off == 0:
            continue
        sh = pltpu.roll(x, (-off) % seq, 0)
        ok = (pos + off >= 0) & (pos + off < seq)
        acc = acc + jnp.where(ok, sh, 0.0) * w[tap:tap + 1, :]
    y = acc * _sigmoid(acc)
    norm_scale = jnp.where(j == 0, B_DK ** -0.5, 1.0).astype(F32)
    for h in range(N_HEADS):
        yh = y[:, h * B_DK:(h + 1) * B_DK]
        ss = jnp.sum(yh * yh, axis=-1, keepdims=True)
        nh = yh * (lax.rsqrt(ss + EPS) * norm_scale)
        o_ref[:, h * B_DK:(h + 1) * B_DK] = jnp.where(j < 2, nh, yh).astype(o_ref.dtype)


def _gdn_prep(proj, conv_w, batch, seq):
    return pl.pallas_call(
        _gdn_prep_kernel,
        grid=(batch, 3),
        in_specs=[pl.BlockSpec((seq, W_GDN), lambda b, j: (b, j)),
                  pl.BlockSpec((B_CONV, W_GDN), lambda b, j: (0, j))],
        out_specs=pl.BlockSpec((seq, W_GDN), lambda b, j: (b, j)),
        out_shape=jax.ShapeDtypeStruct((batch * seq, 3 * W_GDN), BF16),
        compiler_params=_cparams(("parallel", "parallel")),
        name="gdn_prep",
    )(proj, conv_w)


def _lane_block(width, block, h):
    lane = lax.broadcasted_iota(jnp.int32, (1, width), 1)
    return (lane >= h * block) & (lane < (h + 1) * block)


def _block_diag(x, block):
    width = x.shape[1]
    zero = jnp.zeros_like(x)
    return jnp.concatenate([jnp.where(_lane_block(width, block, h), x, zero) for h in range(N_HEADS)], axis=0)


def _unit_tri_inverse_cat(a_list, eye_t):
    p = [eye_t - a for a in a_list]
    ab = [a.astype(BF16) for a in a_list]
    ap = [_dot(x, _block_diag(x, CHUNK)) for x in ab]
    for lvl in range(5):
        apb = [x.astype(BF16) for x in ap]
        bd = [_block_diag(x, CHUNK) for x in apb]
        if lvl < 4:
            r = [_dot(jnp.concatenate([pi.astype(BF16), xi], axis=0), bi) for pi, xi, bi in zip(p, apb, bd)]
            p = [pi + ri[0:CHUNK] for pi, ri in zip(p, r)]
            ap = [ri[CHUNK:2 * CHUNK] for ri in r]
        else:
            p = [pi + _dot(pi.astype(BF16), bi) for pi, bi in zip(p, bd)]
    return p


def _chunk_cumsum(g, rev):
    ridx = lax.broadcasted_iota(jnp.int32, g.shape, 0)
    x = g
    s = 1
    while s < CHUNK:
        if rev:
            sh = jnp.where(ridx < CHUNK - s, pltpu.roll(x, CHUNK - s, 0), 0.0)
        else:
            sh = jnp.where(ridx >= s, pltpu.roll(x, s, 0), 0.0)
        x = x + sh
        s *= 2
    return x


def _gdn_chunk_operands(qkv_ref, ab_ref, alog, dtb, rows, rev):
    c0 = N_HEADS if rev else 0
    ab = ab_ref[rows, :]
    xg = ab + dtb
    softplus = jnp.maximum(xg, 0.0) + jnp.log(1.0 + jnp.exp(-jnp.abs(xg)))
    g = -jnp.exp(alog) * softplus
    beta = _sigmoid(ab)
    gc = _chunk_cumsum(g, rev)
    g_last = gc[0:1, :] if rev else gc[CHUNK - 1:CHUNK, :]
    eg = jnp.exp(gc)
    e_rest = jnp.exp(g_last - gc)

    wcat = N_HEADS * CHUNK
    ii = lax.broadcasted_iota(jnp.int32, (CHUNK, 1), 0)
    jl = lax.broadcasted_iota(jnp.int32, (1, wcat), 1) & (CHUNK - 1)
    eye_b = ii == jl
    incl = (ii <= jl) if rev else (ii >= jl)
    g_col = jnp.zeros((CHUNK, wcat), F32)
    for h in range(N_HEADS):
        g_col = jnp.where(_lane_block(wcat, CHUNK, h), gc[:, c0 + h:c0 + h + 1], g_col)
    g_row = jnp.sum(jnp.where(eye_b, g_col, 0.0), axis=0, keepdims=True)
    dec = jnp.exp(jnp.where(incl, g_col - g_row, NEG))

    kb, vb, kbg, qg, kd = [], [], [], [], []
    for h in range(N_HEADS):
        c = c0 + h
        qh = qkv_ref[rows, h * B_DK:(h + 1) * B_DK].astype(F32)
        kh = qkv_ref[rows, W_GDN + h * B_DK:W_GDN + (h + 1) * B_DK].astype(F32)
        vh = qkv_ref[rows, 2 * W_GDN + h * B_DK:2 * W_GDN + (h + 1) * B_DK].astype(F32)
        bcol = beta[:, 8 + c:9 + c]
        egc = eg[:, c:c + 1]
        kbh = kh * bcol
        kb.append(kbh)
        vb.append(vh * bcol)
        kbg.append(kbh * egc)
        qg.append(qh * egc)
        kd.append((kh * e_rest[:, c:c + 1]).astype(BF16))
    cat = lambda xs: jnp.concatenate(xs, axis=1)
    return dict(dec=dec, eye_b=eye_b, kb=cat(kb).astype(BF16), vb=cat(vb).astype(BF16), kbg=cat(kbg).astype(BF16),
                qg=cat(qg), kd=kd, eg_last=jnp.exp(g_last),
                k=qkv_ref[rows, W_GDN:2 * W_GDN], q=qkv_ref[rows, 0:W_GDN])


def _gdn_solve_chunks(ops):
    kq = [_dot_nt(jnp.concatenate([o["kb"], o["q"]], axis=0), _block_diag(o["k"], B_DK)) for o in ops]
    a_cat = [jnp.where(o["eye_b"], 0.0, x[0:CHUNK] * o["dec"]) for o, x in zip(ops, kq)]
    qk = [(x[CHUNK:2 * CHUNK] * o["dec"]).astype(BF16) for o, x in zip(ops, kq)]
    eye_t = jnp.where(ops[0]["eye_b"], 1.0, 0.0).astype(F32)
    t_cat = [t.astype(BF16) for t in _unit_tri_inverse_cat(a_cat, eye_t)]
    ub = [_dot(t, _block_diag(o["vb"], B_DK)).astype(BF16) for o, t in zip(ops, t_cat)]
    wb = [_dot(t, _block_diag(o["kbg"], B_DK)).astype(BF16) for o, t in zip(ops, t_cat)]
    q_eff = [o["qg"] - _dot(x, _block_diag(w, B_DK)) for o, x, w in zip(ops, qk, wb)]
    o_intra = [_dot(x, _block_diag(u, B_DK)) for x, u in zip(qk, ub)]
    mc = []
    for o, u, w in zip(ops, ub, wb):
        per_head = []
        for h in range(N_HEADS):
            cols = slice(h * B_DK, (h + 1) * B_DK)
            per_head.append(_dot_tn(o["kd"][h], jnp.concatenate([w[:, cols], u[:, cols]], axis=1)))
        mc.append(per_head)
    return mc, q_eff, o_intra


def _gdn_kernel(qkv_ref, ab_ref, alog_ref, dtb_ref, o_ref, s_ref, m_ref, c_ref, qe_ref, oi_ref, egl_ref, *, rev, group):
    seq = qkv_ref.shape[0]
    nc = seq // CHUNK
    c0 = N_HEADS if rev else 0
    alog = alog_ref[...]
    dtb = dtb_ref[...]

    def solve(i, carry):
        ns = [i * group + j for j in range(group)]
        rows = [pl.ds(pl.multiple_of(n * CHUNK, CHUNK), CHUNK) for n in ns]
        ops = [_gdn_chunk_operands(qkv_ref, ab_ref, alog, dtb, r, rev) for r in rows]
        mc, q_eff, o_intra = _gdn_solve_chunks(ops)
        for j, n in enumerate(ns):
            for h in range(N_HEADS):
                m_ref[n * N_HEADS + h] = mc[j][h][:, 0:B_DK].astype(BF16)
                c_ref[n * N_HEADS + h] = mc[j][h][:, B_DK:2 * B_DK]
            qe_ref[rows[j], :] = q_eff[j].astype(BF16)
            oi_ref[rows[j], :] = o_intra[j]
            egl_ref[n] = jnp.broadcast_to(ops[j]["eg_last"], (8, 128))
        return carry

    lax.fori_loop(0, nc // group, solve, 0)

    s_ref[...] = jnp.zeros(s_ref.shape, F32)

    def scan(i, carry):
        n = (nc - 1 - i) if rev else i
        rows = pl.ds(pl.multiple_of(n * CHUNK, CHUNK), CHUNK)
        egl = egl_ref[n]
        new_s, new_o = [], []
        for h in range(N_HEADS):
            cols = slice(h * B_DK, (h + 1) * B_DK)
            state = s_ref[h]
            lhs = jnp.concatenate([m_ref[n * N_HEADS + h], qe_ref[rows, cols]], axis=0)
            r = _dot(lhs, state.astype(BF16))
            new_s.append(state * egl[0:1, c0 + h:c0 + h + 1] - r[0:B_DK] + c_ref[n * N_HEADS + h])
            new_o.append(oi_ref[rows, cols] + r[B_DK:B_DK + CHUNK])
        for h in range(N_HEADS):
            s_ref[h] = new_s[h]
            o_ref[rows, h * B_DK:(h + 1) * B_DK] = new_o[h].astype(o_ref.dtype)
        return carry

    lax.fori_loop(0, nc, scan, 0)


def _gdn(qkv, ab, alog, dtb, batch, seq, rev, group=8):
    nc = seq // CHUNK
    return pl.pallas_call(
        functools.partial(_gdn_kernel, rev=rev, group=group),
        grid=(batch,),
        in_specs=[pl.BlockSpec((seq, 3 * W_GDN), lambda b: (b, 0)),
                  pl.BlockSpec((seq, 128), lambda b: (b, 0)),
                  pl.BlockSpec((1, 128), lambda b: (0, 0)),
                  pl.BlockSpec((1, 128), lambda b: (0, 0))],
        out_specs=pl.BlockSpec((seq, W_GDN), lambda b: (b, 0)),
        out_shape=jax.ShapeDtypeStruct((batch * seq, W_GDN), BF16),
        scratch_shapes=[pltpu.VMEM((N_HEADS, B_DK, B_DK), F32),
                        pltpu.VMEM((nc * N_HEADS, B_DK, B_DK), BF16),
                        pltpu.VMEM((nc * N_HEADS, B_DK, B_DK), F32),
                        pltpu.VMEM((seq, W_GDN), BF16),
                        pltpu.VMEM((seq, W_GDN), F32),
                        pltpu.VMEM((nc, 8, 128), F32)],
        compiler_params=_cparams(("parallel",)),
        name="gdn_bwd" if rev else "gdn_fwd",
    )(qkv, ab, alog, dtb)


def _outproj_kernel(x_ref, oa_ref, of_ref, ob_ref, z_ref, oc_ref, onorm_ref, wa_ref, wb_ref, wc_ref, o_ref):
    acc = x_ref[...] + _dot(oa_ref[...], wa_ref[...])
    acc = acc + _dot(oc_ref[...], wc_ref[...])
    for h in range(N_HEADS):
        cols = slice(h * B_DK, (h + 1) * B_DK)
        o = of_ref[:, cols].astype(F32) + ob_ref[:, cols].astype(F32)
        ms = jnp.mean(o * o, axis=-1, keepdims=True)
        y = (o * lax.rsqrt(ms + EPS)) * onorm_ref[...]
        z = z_ref[:, cols].astype(F32)
        gated = (y * (z * _sigmoid(z))).astype(BF16)
        acc = acc + _dot(gated, wb_ref[h * B_DK:(h + 1) * B_DK, :])
    o_ref[...] = acc


def _out_proj(x, oa, o_f, o_b, proj, oc, onorm, wa, wb, wc, tm=512):
    t, d = x.shape
    row = lambda w: pl.BlockSpec((tm, w), lambda i: (i, 0))
    full = lambda w: pl.BlockSpec((w, d), lambda i: (0, 0))
    return pl.pallas_call(
        _outproj_kernel,
        grid=(t // tm,),
        in_specs=[row(d), row(W_ATT), row(W_GDN), row(W_GDN),
                  pl.BlockSpec((tm, W_GDN), lambda i: (i, COL_BZ // W_GDN)),
                  row(W_ATT), pl.BlockSpec((1, B_DK), lambda i: (0, 0)),
                  full(W_ATT), full(W_GDN), full(W_ATT)],
        out_specs=row(d),
        out_shape=jax.ShapeDtypeStruct((t, d), F32),
        compiler_params=_cparams(("parallel",)),
        name="out_proj",
    )(x, oa, o_f, o_b, proj, oc, onorm, wa, wb, wc)


def _ffn_kernel(x_ref, nw_ref, wg_ref, wu_ref, wd_ref, o_ref, xn_ref, acc_ref):
    j = pl.program_id(1)

    @pl.when(j == 0)
    def _():
        x = x_ref[...]
        ms = jnp.mean(x * x, axis=-1, keepdims=True)
        xn_ref[...] = ((x * lax.rsqrt(ms + EPS)) * nw_ref[...]).astype(BF16)
        acc_ref[...] = x

    xn = xn_ref[...]
    gate = _dot(xn, wg_ref[...])
    up = _dot(xn, wu_ref[...])
    hidden = (gate * _sigmoid(gate)) * up
    acc_ref[...] += _dot(hidden.astype(BF16), wd_ref[...])

    @pl.when(j == pl.num_programs(1) - 1)
    def _():
        o_ref[...] = acc_ref[...]


def _ffn(x, nw, w_gate_up, w_down, tm=512, tf=1408):
    t, d = x.shape
    dff = w_down.shape[0]
    nf = dff // tf
    return pl.pallas_call(
        _ffn_kernel,
        grid=(t // tm, nf),
        in_specs=[pl.BlockSpec((tm, d), lambda i, j: (i, 0)),
                  pl.BlockSpec((1, d), lambda i, j: (0, 0)),
                  pl.BlockSpec((d, tf), lambda i, j: (0, j)),
                  pl.BlockSpec((d, tf), lambda i, j: (0, j + nf)),
                  pl.BlockSpec((tf, d), lambda i, j: (j, 0))],
        out_specs=pl.BlockSpec((tm, d), lambda i, j: (i, 0)),
        out_shape=jax.ShapeDtypeStruct((t, d), F32),
        scratch_shapes=[pltpu.VMEM((tm, d), BF16), pltpu.VMEM((tm, d), F32)],
        compiler_params=_cparams(("parallel", "arbitrary")),
        name="ffn",
    )(x, nw, w_gate_up, w_gate_up, w_down)


def _rope_tables(seq):
    t = np.arange(seq, dtype=np.float64)

    def build(groups):
        cos = np.ones((seq, HEAD_DIM))
        s_lo = np.zeros((seq, HEAD_DIM))
        s_hi = np.zeros((seq, HEAD_DIM))
        for start, half, theta, pos in groups:
            inv = np.float64(np.float32(theta)) ** (-np.arange(half, dtype=np.float64) / half)
            ang = pos[:, None] * inv[None, :]
            cos[:, start:start + half] = np.cos(ang)
            cos[:, start + half:start + 2 * half] = np.cos(ang)
            s_lo[:, start:start + half] = -np.sin(ang)
            s_hi[:, start + half:start + 2 * half] = np.sin(ang)
        tab = np.stack([cos, s_lo, s_hi])
        return np.tile(tab, (1, 1, N_HEADS)).astype(np.float32)

    tab_a = build([(0, A_ROT_HALF, A_THETA, t)])
    row = np.floor(t / GRID_W)
    col = t - row * GRID_W
    tab_c = build([(0, C_ROT_HALF, C_THETA, row), (2 * C_ROT_HALF, C_ROT_HALF, C_THETA, col)])
    by_class = lambda tab: tab.reshape(3, seq // N_CLS, N_CLS, -1).transpose(0, 2, 1, 3).reshape(3, seq, -1)
    return by_class(tab_a), by_class(tab_c)


def _multiplicity_tables(seq):
    (w0, d0), rest = A_PATTERNS[0], A_PATTERNS[1:]
    assert d0 == 1 and all(d % N_CLS == 0 for _, d in rest)
    r0 = w0 // 2
    tq, win = A_LOCAL_TQ, A_LOCAL_TQ + 2 * r0
    qi = np.arange(tq)[:, None]
    kj = np.arange(win)[None, :]
    local = np.stack([(np.abs(qi + off - kj) <= r0) for off in (0, r0, 2 * r0)]).astype(np.float32)
    n = seq // N_CLS
    d = np.arange(n)[:, None] - np.arange(n)[None, :]
    dilated = np.zeros((n, n), np.float32)
    for window, dil in rest:
        step = dil // N_CLS
        radius = window // (2 * dil)
        dilated += ((d % step == 0) & (np.abs(d) <= radius * step)).astype(np.float32)
    return local, dilated


def _head_mean_matrix():
    e = np.kron(np.eye(N_HEADS), np.full((HEAD_DIM, HEAD_DIM), 1.0 / HEAD_DIM))
    return e.astype(np.float32)


def _pad_lanes(v, width):
    return jnp.pad(v.reshape(1, -1), ((0, 0), (0, width - v.size)))


def kernel(x, norm1, w_in, qn_a, kn_a, conv_b, a_log_b, dt_bias_b, onorm_b, qn_c, kn_c, w_out, norm2, w_gate_up, w_down):
    batch, seq, d = x.shape
    depth = w_in.shape[0]
    t = batch * seq

    tab_a, tab_c = _rope_tables(seq)
    tab_a, tab_c = jnp.asarray(tab_a), jnp.asarray(tab_c)
    mult_local, mult_dil = _multiplicity_tables(seq)
    mult_local, mult_dil = jnp.asarray(mult_local, dtype=BF16), jnp.asarray(mult_dil, dtype=BF16)
    e_mat = jnp.asarray(_head_mean_matrix(), dtype=BF16)

    c0 = 3 * W_ATT
    c1 = c0 + 4 * W_GDN
    c2 = c1 + 4 * N_HEADS
    cq = [w_in[:, :, c2 + h * HEAD_DIM:c2 + (h + 1) * HEAD_DIM] for h in C_HEAD_ORDER]
    w_in_p = jnp.concatenate(
        [w_in[:, :, c0:c1], w_in[:, :, 0:c0]] + cq + [w_in[:, :, c2 + W_ATT:], w_in[:, :, c1:c2]], axis=-1)
    w_in_p = jnp.pad(w_in_p, ((0, 0), (0, 0), (0, IN_PAD - w_in_p.shape[-1]))).astype(BF16)

    w_out_a = w_out[:, 0:W_ATT].astype(BF16)
    w_out_b = w_out[:, W_ATT:W_ATT + W_GDN].astype(BF16)
    wc = w_out[:, W_ATT + W_GDN:]
    w_out_c = jnp.concatenate([wc[:, h * HEAD_DIM:(h + 1) * HEAD_DIM] for h in C_HEAD_ORDER], axis=1).astype(BF16)
    w_gu = w_gate_up.astype(BF16)
    w_dn = w_down.astype(BF16)

    tile4 = lambda v: jnp.tile(v.reshape(1, -1), (1, N_HEADS))

    xt = x.reshape(t, d)
    for i in range(depth):
        proj, proj_ab = _in_proj(xt, norm1[i].reshape(1, d), w_in_p[i])
        (qa_cls, ka_cls, va_cls), (qa, ka, va, qc, kc, vc) = _attn_prep(
            proj, tile4(qn_a[i]), tile4(kn_a[i]), tile4(qn_c[i]), tile4(kn_c[i]), tab_a, tab_c, e_mat, batch, seq)
        o1, lse1 = _attn_a_local(qa, ka, va, mult_local, batch, seq)
        o_a = _attn_a_dilated(qa_cls, ka_cls, va_cls, mult_dil, o1, lse1, batch, seq)
        o_c = _attn_c(qc, kc, vc, batch, seq)
        qkv_b = _gdn_prep(proj, conv_b[i], batch, seq)
        alog, dtb = _pad_lanes(a_log_b[i], 128), _pad_lanes(dt_bias_b[i], 128)
        o_f = _gdn(qkv_b, proj_ab, alog, dtb, batch, seq, rev=False)
        o_r = _gdn(qkv_b, proj_ab, alog, dtb, batch, seq, rev=True)
        xt = _out_proj(xt, o_a, o_f, o_r, proj, o_c, onorm_b[i].reshape(1, B_DK), w_out_a[i], w_out_b[i], w_out_c[i])
        xt = _ffn(xt, norm2[i].reshape(1, d), w_gu[i], w_dn[i])
    return xt.reshape(batch, seq, d)
```

```python
import functools

import numpy as np
import jax
import jax.numpy as jnp
from jax import lax
from jax.experimental import pallas as pl
from jax.experimental.pallas import tpu as pltpu

F32 = jnp.float32
BF16 = jnp.bfloat16

EPS = 1e-6
GRID_W = 64
HEAD_DIM = 64
N_HEADS = 4
A_PATTERNS = ((128, 1), (512, 4), (2048, 16))
A_ROT_HALF = 8
A_THETA = 500000.0
C_ROT_HALF = 16
C_THETA = 10000.0
B_DK = 128
B_CONV = 5
CHUNK = 64
NEG = -1e30

W_ATT = N_HEADS * HEAD_DIM
W_GDN = N_HEADS * B_DK
COL_BQ, COL_BK, COL_BV, COL_BZ = 0, 512, 1024, 1536
COL_AQ, COL_AK, COL_AV = 2048, 2304, 2560
COL_CQ, COL_CK, COL_CV = 2816, 3072, 3200
COL_AB = 3328
IN_PAD = 3456
N_CLS = 4
A_LOCAL_TQ = 128
C_HEAD_ORDER = (0, 2, 1, 3)

VMEM_LIMIT = 56 * 1024 * 1024


def _cparams(sem):
    return pltpu.CompilerParams(dimension_semantics=sem, vmem_limit_bytes=VMEM_LIMIT)


def _split2(x):
    hi = x.astype(BF16)
    lo = (x - hi.astype(F32)).astype(BF16)
    return hi, lo


def _dot(a, b):
    return jnp.dot(a, b, preferred_element_type=F32)


def _dot_nt(a, b):
    return lax.dot_general(a, b, (((1,), (1,)), ((), ())), preferred_element_type=F32)


def _dot_tn(a, b):
    return lax.dot_general(a, b, (((0,), (0,)), ((), ())), preferred_element_type=F32)


def _sigmoid(x):
    return 1.0 / (1.0 + jnp.exp(-x))


def _silu(x):
    hx = 0.5 * x
    return hx + hx * jnp.tanh(hx)


def _inproj_kernel(x_ref, nw_ref, w_ref, o_ref, oab_ref):
    x = x_ref[...]
    ms = jnp.mean(x * x, axis=-1, keepdims=True)
    y = (x * lax.rsqrt(ms + EPS)) * nw_ref[...]
    res = _dot(y.astype(BF16), w_ref[...])
    o_ref[...] = res[:, 0:COL_AB].astype(BF16)
    oab_ref[...] = res[:, COL_AB:IN_PAD]


def _in_proj(x, nw, w, tm=256):
    t, d = x.shape
    n = w.shape[1]
    return pl.pallas_call(
        _inproj_kernel,
        grid=(t // tm,),
        in_specs=[pl.BlockSpec((tm, d), lambda i: (i, 0)),
                  pl.BlockSpec((1, d), lambda i: (0, 0)),
                  pl.BlockSpec((d, n), lambda i: (0, 0))],
        out_specs=[pl.BlockSpec((tm, COL_AB), lambda i: (i, 0)),
                   pl.BlockSpec((tm, n - COL_AB), lambda i: (i, 0))],
        out_shape=[jax.ShapeDtypeStruct((t, COL_AB), BF16), jax.ShapeDtypeStruct((t, n - COL_AB), F32)],
        compiler_params=_cparams(("parallel",)),
        name="in_proj",
    )(x, nw, w)


def _head_norm_rope(x, w, e_hi, cos, sin_lo, sin_hi, rot_half, scale):
    xx = x * x
    hi, lo = _split2(xx)
    ms = _dot(hi, e_hi) + _dot(lo, e_hi)
    y = (x * lax.rsqrt(ms + EPS)) * w
    width = x.shape[1]
    up = pltpu.roll(y, width - rot_half, 1)
    dn = pltpu.roll(y, rot_half, 1)
    out = y * cos + up * sin_lo + dn * sin_hi
    return out * scale


def _join_halves(halves):
    return jnp.concatenate(halves, axis=1)


def _attn_prep_kernel(aq_ref, ak_ref, av_ref, cq_ref, ck_ref, cv_ref,
                      qna_ref, kna_ref, qnc_ref, knc_ref, ta_ref, tc_ref, e_ref,
                      oqa_cls_ref, oka_cls_ref, ova_cls_ref, oqa_ref, oka_ref, ova_ref, oqc_ref, okc_ref, ovc_ref,
                      scr_ref):
    f32 = lambda r: r[...].astype(F32)
    e = e_ref[...]
    cos_a, sl_a, sh_a = ta_ref[0], ta_ref[1], ta_ref[2]
    cos_c, sl_c, sh_c = tc_ref[0], tc_ref[1], tc_ref[2]
    scale = HEAD_DIM ** -0.5
    rows = scr_ref.shape[1] // N_CLS

    def emit(x, tok_ref, cls_ref):
        tok_ref[...] = x.astype(BF16)
        for j in range(2):
            scr_ref[j] = x[:, j * 128:(j + 1) * 128]
        for c in range(N_CLS):
            cls_ref[c] = _join_halves([scr_ref[j, pl.ds(c, rows, stride=N_CLS), :] for j in range(2)]).astype(BF16)

    emit(_head_norm_rope(f32(aq_ref), qna_ref[...], e, cos_a, sl_a, sh_a, A_ROT_HALF, scale), oqa_ref, oqa_cls_ref)
    emit(_head_norm_rope(f32(ak_ref), kna_ref[...], e, cos_a, sl_a, sh_a, A_ROT_HALF, 1.0), oka_ref, oka_cls_ref)
    emit(f32(av_ref), ova_ref, ova_cls_ref)
    oqc_ref[...] = _head_norm_rope(f32(cq_ref), qnc_ref[...], e, cos_c, sl_c, sh_c, C_ROT_HALF, scale).astype(BF16)
    okc_ref[...] = _head_norm_rope(f32(ck_ref), knc_ref[...][:, :128], e[:128, :128],
                                   cos_c[:, :128], sl_c[:, :128], sh_c[:, :128], C_ROT_HALF, 1.0).astype(BF16)
    ovc_ref[...] = cv_ref[...]


def _attn_prep(proj, qna, kna, qnc, knc, tab_a, tab_c, e_mat, batch, seq, tm=512):
    t = proj.shape[0]
    ns = seq // tm
    cb = lambda col, w: pl.BlockSpec((tm, w), lambda i, c=col // w: (i, c))
    vec = pl.BlockSpec((1, W_ATT), lambda i: (0, 0))
    tab = pl.BlockSpec((3, tm, W_ATT), lambda i: (0, i % ns, 0))
    ob = lambda w: pl.BlockSpec((tm, w), lambda i: (i, 0))
    cls = pl.BlockSpec((None, N_CLS, tm // N_CLS, W_ATT), lambda i: (i // ns, 0, i % ns, 0))
    widths = (256, 256, 256, 256, 128, 128)
    outs = pl.pallas_call(
        _attn_prep_kernel,
        grid=(t // tm,),
        in_specs=[cb(COL_AQ, 256), cb(COL_AK, 256), cb(COL_AV, 256),
                  cb(COL_CQ, 256), cb(COL_CK, 128), cb(COL_CV, 128),
                  vec, vec, vec, vec, tab, tab,
                  pl.BlockSpec((W_ATT, W_ATT), lambda i: (0, 0))],
        out_specs=[cls, cls, cls] + [ob(w) for w in widths],
        out_shape=[jax.ShapeDtypeStruct((batch, N_CLS, seq // N_CLS, W_ATT), BF16)] * 3
                  + [jax.ShapeDtypeStruct((t, w), BF16) for w in widths],
        scratch_shapes=[pltpu.VMEM((2, tm, 128), F32)],
        compiler_params=_cparams(("parallel",)),
        name="attn_prep",
    )(proj, proj, proj, proj, proj, proj, qna, kna, qnc, knc, tab_a, tab_c, e_mat)
    return outs[:3], outs[3:]


def _head_masks(width):
    lane = lax.broadcasted_iota(jnp.int32, (1, width), 1)
    return [(lane >= h * HEAD_DIM) & (lane < (h + 1) * HEAD_DIM) for h in range(width // HEAD_DIM)]


def _attn_a_local_kernel(q_ref, k_ref, v_ref, m_ref, o1_ref, lse_ref):
    i = pl.program_id(1)
    seq = k_ref.shape[0]
    tq, win = m_ref.shape[1], m_ref.shape[2]
    nblk = q_ref.shape[0] // tq
    last = seq // tq - 1
    hms = _head_masks(W_ATT)
    vs, scores = [], []
    for j in range(nblk):
        blk = i * nblk + j
        start = jnp.clip(blk * tq - (win - tq) // 2, 0, seq - win)
        rows = pl.ds(pl.multiple_of(start, 64), win)
        k = k_ref[rows, :]
        vs.append(v_ref[rows, :])
        mult = m_ref[1]
        if j == 0:
            mult = jnp.where(blk == 0, m_ref[0], mult)
        if j == nblk - 1:
            mult = jnp.where(blk == last, m_ref[2], mult)
        q = q_ref[j * tq:(j + 1) * tq, :]
        scores.append([jnp.where(mult > 0, _dot_nt(jnp.where(hm, q, jnp.zeros_like(q)), k), NEG) for hm in hms])
    maxes = [[jnp.max(s, axis=-1, keepdims=True) for s in row] for row in scores]
    probs = [[jnp.exp(s - m) for s, m in zip(srow, mrow)] for srow, mrow in zip(scores, maxes)]
    sums = [[jnp.sum(p, axis=-1, keepdims=True) for p in row] for row in probs]
    pvs = [[_dot(p.astype(BF16), v) for p in row] for row, v in zip(probs, vs)]
    for j in range(nblk):
        out = jnp.zeros((tq, W_ATT), F32)
        lse = jnp.zeros((tq, W_ATT), F32)
        for h in range(N_HEADS):
            out = jnp.where(hms[h], pvs[j][h] * (1.0 / sums[j][h]), out)
            lse = jnp.where(hms[h], maxes[j][h] + jnp.log(sums[j][h]), lse)
        for half in range(2):
            o1_ref[half, j * tq:(j + 1) * tq, :] = out[:, half * 128:(half + 1) * 128]
            lse_ref[half, j * tq:(j + 1) * tq, :] = lse[:, half * 128:(half + 1) * 128]


def _attn_a_local(q, k, v, mult, batch, seq, nblk=4):
    tq = mult.shape[1]
    rows = nblk * tq
    nq = seq // rows
    return pl.pallas_call(
        _attn_a_local_kernel,
        grid=(batch, nq),
        in_specs=[pl.BlockSpec((rows, W_ATT), lambda b, i: (b * nq + i, 0)),
                  pl.BlockSpec((seq, W_ATT), lambda b, i: (b, 0)),
                  pl.BlockSpec((seq, W_ATT), lambda b, i: (b, 0)),
                  pl.BlockSpec(mult.shape, lambda b, i: (0, 0, 0))],
        out_specs=[pl.BlockSpec((2, rows, 128), lambda b, i: (0, b * nq + i, 0)),
                   pl.BlockSpec((2, rows, 128), lambda b, i: (0, b * nq + i, 0))],
        out_shape=[jax.ShapeDtypeStruct((2, batch * seq, 128), F32), jax.ShapeDtypeStruct((2, batch * seq, 128), F32)],
        compiler_params=_cparams(("parallel", "parallel")),
        name="attn_a_local",
    )(q, k, v, mult)


def _attn_a_dilated_kernel(q_ref, k_ref, v_ref, m_ref, o1_ref, lse1_ref, o_ref):
    tq = q_ref.shape[1]
    mult = m_ref[...]
    multf = mult.astype(F32)
    hms = _head_masks(W_ATT)
    qs = [q_ref[c] for c in range(N_CLS)]
    scores = [[jnp.where(mult > 0, _dot_nt(jnp.where(hm, q, jnp.zeros_like(q)), k_ref[c]), NEG) for hm in hms]
              for c, q in enumerate(qs)]
    m2 = [[jnp.max(s, axis=-1, keepdims=True) for s in row] for row in scores]
    probs = [[jnp.exp(s - m) * multf for s, m in zip(srow, mrow)] for srow, mrow in zip(scores, m2)]
    l2 = [[jnp.sum(p, axis=-1, keepdims=True) for p in row] for row in probs]
    pv2 = [[_dot(p.astype(BF16), v_ref[c]) for p in row] for c, row in enumerate(probs)]
    for c in range(N_CLS):
        m2b = jnp.zeros((tq, W_ATT), F32)
        l2b = jnp.zeros((tq, W_ATT), F32)
        pv2b = jnp.zeros((tq, W_ATT), F32)
        for h in range(N_HEADS):
            m2b = jnp.where(hms[h], m2[c][h], m2b)
            l2b = jnp.where(hms[h], l2[c][h], l2b)
            pv2b = jnp.where(hms[h], pv2[c][h], pv2b)
        tok = pl.ds(c, tq, stride=N_CLS)
        lse1 = _join_halves([lse1_ref[half, tok, :] for half in range(2)])
        o1 = _join_halves([o1_ref[half, tok, :] for half in range(2)])
        m = jnp.maximum(lse1, m2b)
        a1 = jnp.exp(lse1 - m)
        a2 = jnp.exp(m2b - m)
        out = (o1 * a1 + pv2b * a2) / (a1 + l2b * a2)
        for half in range(2):
            o_ref[half, tok, :] = out[:, half * 128:(half + 1) * 128]


def _attn_a_dilated(q, k, v, mult, o1, lse1, batch, seq, tq=128):
    rows = seq // N_CLS
    nu = rows // tq
    tok = pl.BlockSpec((2, N_CLS * tq, 128), lambda u, b: (0, b * nu + u, 0))
    return pl.pallas_call(
        _attn_a_dilated_kernel,
        grid=(nu, batch),
        in_specs=[pl.BlockSpec((None, N_CLS, tq, W_ATT), lambda u, b: (b, 0, u, 0)),
                  pl.BlockSpec((None, N_CLS, rows, W_ATT), lambda u, b: (b, 0, 0, 0)),
                  pl.BlockSpec((None, N_CLS, rows, W_ATT), lambda u, b: (b, 0, 0, 0)),
                  pl.BlockSpec((tq, rows), lambda u, b: (u, 0)),
                  tok, tok],
        out_specs=tok,
        out_shape=jax.ShapeDtypeStruct((2, batch * seq, 128), F32),
        compiler_params=_cparams(("parallel", "parallel")),
        name="attn_a_dilated",
    )(q, k, v, mult, o1, lse1)


def _attn_c_kernel(q_ref, k_ref, v_ref, o_ref):
    k = k_ref[...]
    v = v_ref[...]
    hms = _head_masks(128)
    qs = [q_ref[:, pair * 128:(pair + 1) * 128] for pair in range(2)]
    scores = [[_dot_nt(jnp.where(hm, q, jnp.zeros_like(q)), k) for hm in hms] for q in qs]
    maxes = [[jnp.max(s, axis=-1, keepdims=True) for s in row] for row in scores]
    probs = [[jnp.exp(s - m) for s, m in zip(srow, mrow)] for srow, mrow in zip(scores, maxes)]
    sums = [[jnp.sum(p, axis=-1, keepdims=True) for p in row] for row in probs]
    pvs = [[_dot(p.astype(BF16), v) for p in row] for row in probs]
    for pair in range(2):
        acc = jnp.zeros(qs[pair].shape, F32)
        for half in range(2):
            acc = jnp.where(hms[half], pvs[pair][half] * (1.0 / sums[pair][half]), acc)
        o_ref[:, pair * 128:(pair + 1) * 128] = acc.astype(o_ref.dtype)


def _attn_c(q, k, v, batch, seq, tq=256):
    nq = seq // tq
    return pl.pallas_call(
        _attn_c_kernel,
        grid=(batch, nq),
        in_specs=[pl.BlockSpec((tq, W_ATT), lambda b, i: (b * nq + i, 0)),
                  pl.BlockSpec((seq, 128), lambda b, i: (b, 0)),
                  pl.BlockSpec((seq, 128), lambda b, i: (b, 0))],
        out_specs=pl.BlockSpec((tq, W_ATT), lambda b, i: (b * nq + i, 0)),
        out_shape=jax.ShapeDtypeStruct((batch * seq, W_ATT), BF16),
        compiler_params=_cparams(("parallel", "parallel")),
        name="attn_c",
    )(q, k, v)


def _gdn_prep_kernel(x_ref, w_ref, o_ref):
    j = pl.program_id(1)
    x = x_ref[...].astype(F32)
    w = w_ref[...]
    seq = x.shape[0]
    pos = lax.broadcasted_iota(jnp.int32, (seq, 1), 0)
    acc = x * w[B_CONV // 2:B_CONV // 2 + 1, :]
    for tap in range(B_CONV):
        off = tap - B_CONV // 2
        if off == 0:
            continue
        sh = pltpu.roll(x, (-off) % seq, 0)
        ok = (pos + off >= 0) & (pos + off < seq)
        acc = acc + jnp.where(ok, sh, 0.0) * w[tap:tap + 1, :]
    y = _silu(acc)
    norm_scale = jnp.where(j == 0, B_DK ** -0.5, 1.0).astype(F32)
    for h in range(N_HEADS):
        yh = y[:, h * B_DK:(h + 1) * B_DK]
        ss = jnp.sum(yh * yh, axis=-1, keepdims=True)
        nh = yh * (lax.rsqrt(ss + EPS) * norm_scale)
        o_ref[:, h * B_DK:(h + 1) * B_DK] = jnp.where(j < 2, nh, yh).astype(o_ref.dtype)


def _gdn_prep(proj, conv_w, batch, seq):
    return pl.pallas_call(
        _gdn_prep_kernel,
        grid=(batch, 3),
        in_specs=[pl.BlockSpec((seq, W_GDN), lambda b, j: (b, j)),
                  pl.BlockSpec((B_CONV, W_GDN), lambda b, j: (0, j))],
        out_specs=pl.BlockSpec((seq, W_GDN), lambda b, j: (b, j)),
        out_shape=jax.ShapeDtypeStruct((batch * seq, 3 * W_GDN), BF16),
        compiler_params=_cparams(("parallel", "parallel")),
        name="gdn_prep",
    )(proj, conv_w)


def _lane_block(width, block, h):
    lane = lax.broadcasted_iota(jnp.int32, (1, width), 1)
    return (lane >= h * block) & (lane < (h + 1) * block)


def _block_diag(x, block):
    width = x.shape[1]
    zero = jnp.zeros_like(x)
    return jnp.concatenate([jnp.where(_lane_block(width, block, h), x, zero) for h in range(N_HEADS)], axis=0)


def _unit_tri_inverse_cat(a_list, eye_t):
    p = [eye_t - a for a in a_list]
    ab = [a.astype(BF16) for a in a_list]
    ap = [_dot(x, _block_diag(x, CHUNK)) for x in ab]
    for lvl in range(5):
        apb = [x.astype(BF16) for x in ap]
        bd = [_block_diag(x, CHUNK) for x in apb]
        if lvl < 4:
            r = [_dot(jnp.concatenate([pi.astype(BF16), xi], axis=0), bi) for pi, xi, bi in zip(p, apb, bd)]
            p = [pi + ri[0:CHUNK] for pi, ri in zip(p, r)]
            ap = [ri[CHUNK:2 * CHUNK] for ri in r]
        else:
            p = [pi + _dot(pi.astype(BF16), bi) for pi, bi in zip(p, bd)]
    return p


def _chunk_cumsum(g, rev):
    ridx = lax.broadcasted_iota(jnp.int32, g.shape, 0)
    x = g
    s = 1
    while s < CHUNK:
        if rev:
            sh = jnp.where(ridx < CHUNK - s, pltpu.roll(x, CHUNK - s, 0), 0.0)
        else:
            sh = jnp.where(ridx >= s, pltpu.roll(x, s, 0), 0.0)
        x = x + sh
        s *= 2
    return x


def _gdn_chunk_operands(qkv_ref, ab_ref, alog, dtb, rows, rev):
    c0 = N_HEADS if rev else 0
    ab = ab_ref[rows, :]
    xg = ab + dtb
    softplus = jnp.maximum(xg, 0.0) + jnp.log(1.0 + jnp.exp(-jnp.abs(xg)))
    g = -jnp.exp(alog) * softplus
    beta = _sigmoid(ab)
    gc = _chunk_cumsum(g, rev)
    g_last = gc[0:1, :] if rev else gc[CHUNK - 1:CHUNK, :]
    eg = jnp.exp(gc)
    e_rest = jnp.exp(g_last - gc)

    wcat = N_HEADS * CHUNK
    ii = lax.broadcasted_iota(jnp.int32, (CHUNK, 1), 0)
    jl = lax.broadcasted_iota(jnp.int32, (1, wcat), 1) & (CHUNK - 1)
    eye_b = ii == jl
    incl = (ii <= jl) if rev else (ii >= jl)
    g_col = jnp.zeros((CHUNK, wcat), F32)
    for h in range(N_HEADS):
        g_col = jnp.where(_lane_block(wcat, CHUNK, h), gc[:, c0 + h:c0 + h + 1], g_col)
    g_row = jnp.sum(jnp.where(eye_b, g_col, 0.0), axis=0, keepdims=True)
    dec = jnp.exp(jnp.where(incl, g_col - g_row, NEG))

    kb, vb, kbg, qg, kd = [], [], [], [], []
    for h in range(N_HEADS):
        c = c0 + h
        qh = qkv_ref[rows, h * B_DK:(h + 1) * B_DK].astype(F32)
        kh = qkv_ref[rows, W_GDN + h * B_DK:W_GDN + (h + 1) * B_DK].astype(F32)
        vh = qkv_ref[rows, 2 * W_GDN + h * B_DK:2 * W_GDN + (h + 1) * B_DK].astype(F32)
        bcol = beta[:, 8 + c:9 + c]
        egc = eg[:, c:c + 1]
        kbh = kh * bcol
        kb.append(kbh)
        vb.append(vh * bcol)
        kbg.append(kbh * egc)
        qg.append(qh * egc)
        kd.append((kh * e_rest[:, c:c + 1]).astype(BF16))
    cat = lambda xs: jnp.concatenate(xs, axis=1)
    return dict(dec=dec, eye_b=eye_b, kb=cat(kb).astype(BF16), vb=cat(vb).astype(BF16), kbg=cat(kbg).astype(BF16),
                qg=cat(qg), kd=kd, eg_last=jnp.exp(g_last),
                k=qkv_ref[rows, W_GDN:2 * W_GDN], q=qkv_ref[rows, 0:W_GDN])


def _gdn_solve_chunks(ops):
    kq = [_dot_nt(jnp.concatenate([o["kb"], o["q"]], axis=0), _block_diag(o["k"], B_DK)) for o in ops]
    a_cat = [jnp.where(o["eye_b"], 0.0, x[0:CHUNK] * o["dec"]) for o, x in zip(ops, kq)]
    qk = [(x[CHUNK:2 * CHUNK] * o["dec"]).astype(BF16) for o, x in zip(ops, kq)]
    eye_t = jnp.where(ops[0]["eye_b"], 1.0, 0.0).astype(F32)
    t_cat = [t.astype(BF16) for t in _unit_tri_inverse_cat(a_cat, eye_t)]
    ub = [_dot(t, _block_diag(o["vb"], B_DK)).astype(BF16) for o, t in zip(ops, t_cat)]
    wb = [_dot(t, _block_diag(o["kbg"], B_DK)).astype(BF16) for o, t in zip(ops, t_cat)]
    q_eff = [o["qg"] - _dot(x, _block_diag(w, B_DK)) for o, x, w in zip(ops, qk, wb)]
    o_intra = [_dot(x, _block_diag(u, B_DK)) for x, u in zip(qk, ub)]
    mc = []
    for o, u, w in zip(ops, ub, wb):
        per_head = []
        for h in range(N_HEADS):
            cols = slice(h * B_DK, (h + 1) * B_DK)
            per_head.append(_dot_tn(o["kd"][h], jnp.concatenate([w[:, cols], u[:, cols]], axis=1)))
        mc.append(per_head)
    return mc, q_eff, o_intra


def _gdn_kernel(qkv_ref, ab_ref, alog_ref, dtb_ref, o_ref, s_ref, m_ref, c_ref, qe_ref, oi_ref, egl_ref, *, rev, group):
    seq = qkv_ref.shape[0]
    nc = seq // CHUNK
    c0 = N_HEADS if rev else 0
    alog = alog_ref[...]
    dtb = dtb_ref[...]

    def solve(i, carry):
        ns = [i * group + j for j in range(group)]
        rows = [pl.ds(pl.multiple_of(n * CHUNK, CHUNK), CHUNK) for n in ns]
        ops = [_gdn_chunk_operands(qkv_ref, ab_ref, alog, dtb, r, rev) for r in rows]
        mc, q_eff, o_intra = _gdn_solve_chunks(ops)
        for j, n in enumerate(ns):
            for h in range(N_HEADS):
                m_ref[n * N_HEADS + h] = mc[j][h][:, 0:B_DK].astype(BF16)
                c_ref[n * N_HEADS + h] = mc[j][h][:, B_DK:2 * B_DK]
            qe_ref[rows[j], :] = q_eff[j].astype(BF16)
            oi_ref[rows[j], :] = o_intra[j]
            egl_ref[n] = jnp.broadcast_to(ops[j]["eg_last"], (8, 128))
        return carry

    lax.fori_loop(0, nc // group, solve, 0)

    s_ref[...] = jnp.zeros(s_ref.shape, F32)

    def scan(i, carry):
        n = (nc - 1 - i) if rev else i
        rows = pl.ds(pl.multiple_of(n * CHUNK, CHUNK), CHUNK)
        egl = egl_ref[n]
        new_s, new_o = [], []
        for h in range(N_HEADS):
            cols = slice(h * B_DK, (h + 1) * B_DK)
            state = s_ref[h]
            lhs = jnp.concatenate([m_ref[n * N_HEADS + h], qe_ref[rows, cols]], axis=0)
            r = _dot(lhs, state.astype(BF16))
            new_s.append(state * egl[0:1, c0 + h:c0 + h + 1] - r[0:B_DK] + c_ref[n * N_HEADS + h])
            new_o.append(oi_ref[rows, cols] + r[B_DK:B_DK + CHUNK])
        for h in range(N_HEADS):
            s_ref[h] = new_s[h]
            o_ref[rows, h * B_DK:(h + 1) * B_DK] = new_o[h].astype(o_ref.dtype)
        return carry

    lax.fori_loop(0, nc, scan, 0)


def _gdn(qkv, ab, alog, dtb, batch, seq, rev, group=8):
    nc = seq // CHUNK
    return pl.pallas_call(
        functools.partial(_gdn_kernel, rev=rev, group=group),
        grid=(batch,),
        in_specs=[pl.BlockSpec((seq, 3 * W_GDN), lambda b: (b, 0)),
                  pl.BlockSpec((seq, 128), lambda b: (b, 0)),
                  pl.BlockSpec((1, 128), lambda b: (0, 0)),
                  pl.BlockSpec((1, 128), lambda b: (0, 0))],
        out_specs=pl.BlockSpec((seq, W_GDN), lambda b: (b, 0)),
        out_shape=jax.ShapeDtypeStruct((batch * seq, W_GDN), BF16),
        scratch_shapes=[pltpu.VMEM((N_HEADS, B_DK, B_DK), F32),
                        pltpu.VMEM((nc * N_HEADS, B_DK, B_DK), BF16),
                        pltpu.VMEM((nc * N_HEADS, B_DK, B_DK), F32),
                        pltpu.VMEM((seq, W_GDN), BF16),
                        pltpu.VMEM((seq, W_GDN), F32),
                        pltpu.VMEM((nc, 8, 128), F32)],
        compiler_params=_cparams(("parallel",)),
        name="gdn_bwd" if rev else "gdn_fwd",
    )(qkv, ab, alog, dtb)


def _outproj_kernel(x_ref, oa_ref, of_ref, ob_ref, z_ref, oc_ref, onorm_ref, wa_ref, wb_ref, wc_ref, o_ref):
    acc = x_ref[...] + _dot(_join_halves([oa_ref[0], oa_ref[1]]).astype(BF16), wa_ref[...])
    acc = acc + _dot(oc_ref[...], wc_ref[...])
    for h in range(N_HEADS):
        cols = slice(h * B_DK, (h + 1) * B_DK)
        o = of_ref[:, cols].astype(F32) + ob_ref[:, cols].astype(F32)
        ms = jnp.mean(o * o, axis=-1, keepdims=True)
        y = (o * lax.rsqrt(ms + EPS)) * onorm_ref[...]
        z = z_ref[:, cols].astype(F32)
        gated = (y * _silu(z)).astype(BF16)
        acc = acc + _dot(gated, wb_ref[h * B_DK:(h + 1) * B_DK, :])
    o_ref[...] = acc


def _out_proj(x, oa, o_f, o_b, proj, oc, onorm, wa, wb, wc, tm=512):
    t, d = x.shape
    row = lambda w: pl.BlockSpec((tm, w), lambda i: (i, 0))
    full = lambda w: pl.BlockSpec((w, d), lambda i: (0, 0))
    return pl.pallas_call(
        _outproj_kernel,
        grid=(t // tm,),
        in_specs=[row(d), pl.BlockSpec((2, tm, 128), lambda i: (0, i, 0)), row(W_GDN), row(W_GDN),
                  pl.BlockSpec((tm, W_GDN), lambda i: (i, COL_BZ // W_GDN)),
                  row(W_ATT), pl.BlockSpec((1, B_DK), lambda i: (0, 0)),
                  full(W_ATT), full(W_GDN), full(W_ATT)],
        out_specs=row(d),
        out_shape=jax.ShapeDtypeStruct((t, d), F32),
        compiler_params=_cparams(("parallel",)),
        name="out_proj",
    )(x, oa, o_f, o_b, proj, oc, onorm, wa, wb, wc)


def _ffn_kernel(x_ref, nw_ref, wg_ref, wu_ref, wd_ref, o_ref, xn_ref, acc_ref):
    j = pl.program_id(1)

    @pl.when(j == 0)
    def _():
        x = x_ref[...]
        ms = jnp.mean(x * x, axis=-1, keepdims=True)
        xn_ref[...] = ((x * lax.rsqrt(ms + EPS)) * nw_ref[...]).astype(BF16)
        acc_ref[...] = x

    xn = xn_ref[...]
    gate = _dot(xn, wg_ref[...])
    up = _dot(xn, wu_ref[...])
    hidden = _silu(gate) * up
    acc_ref[...] += _dot(hidden.astype(BF16), wd_ref[...])

    @pl.when(j == pl.num_programs(1) - 1)
    def _():
        o_ref[...] = acc_ref[...]


def _ffn(x, nw, w_gate_up, w_down, tm=512, tf=1408):
    t, d = x.shape
    dff = w_down.shape[0]
    nf = dff // tf
    return pl.pallas_call(
        _ffn_kernel,
        grid=(t // tm, nf),
        in_specs=[pl.BlockSpec((tm, d), lambda i, j: (i, 0)),
                  pl.BlockSpec((1, d), lambda i, j: (0, 0)),
                  pl.BlockSpec((d, tf), lambda i, j: (0, j)),
                  pl.BlockSpec((d, tf), lambda i, j: (0, j + nf)),
                  pl.BlockSpec((tf, d), lambda i, j: (j, 0))],
        out_specs=pl.BlockSpec((tm, d), lambda i, j: (i, 0)),
        out_shape=jax.ShapeDtypeStruct((t, d), F32),
        scratch_shapes=[pltpu.VMEM((tm, d), BF16), pltpu.VMEM((tm, d), F32)],
        compiler_params=_cparams(("parallel", "arbitrary")),
        name="ffn",
    )(x, nw, w_gate_up, w_gate_up, w_down)


def _rope_tables(seq):
    t = np.arange(seq, dtype=np.float64)

    def build(groups):
        cos = np.ones((seq, HEAD_DIM))
        s_lo = np.zeros((seq, HEAD_DIM))
        s_hi = np.zeros((seq, HEAD_DIM))
        for start, half, theta, pos in groups:
            inv = np.float64(np.float32(theta)) ** (-np.arange(half, dtype=np.float64) / half)
            ang = pos[:, None] * inv[None, :]
            cos[:, start:start + half] = np.cos(ang)
            cos[:, start + half:start + 2 * half] = np.cos(ang)
            s_lo[:, start:start + half] = -np.sin(ang)
            s_hi[:, start + half:start + 2 * half] = np.sin(ang)
        tab = np.stack([cos, s_lo, s_hi])
        return np.tile(tab, (1, 1, N_HEADS)).astype(np.float32)

    tab_a = build([(0, A_ROT_HALF, A_THETA, t)])
    row = np.floor(t / GRID_W)
    col = t - row * GRID_W
    tab_c = build([(0, C_ROT_HALF, C_THETA, row), (2 * C_ROT_HALF, C_ROT_HALF, C_THETA, col)])
    return tab_a, tab_c


def _multiplicity_tables(seq):
    (w0, d0), rest = A_PATTERNS[0], A_PATTERNS[1:]
    assert d0 == 1 and all(d % N_CLS == 0 for _, d in rest)
    r0 = w0 // 2
    tq, win = A_LOCAL_TQ, A_LOCAL_TQ + 2 * r0
    qi = np.arange(tq)[:, None]
    kj = np.arange(win)[None, :]
    local = np.stack([(np.abs(qi + off - kj) <= r0) for off in (0, r0, 2 * r0)]).astype(np.float32)
    n = seq // N_CLS
    d = np.arange(n)[:, None] - np.arange(n)[None, :]
    dilated = np.zeros((n, n), np.float32)
    for window, dil in rest:
        step = dil // N_CLS
        radius = window // (2 * dil)
        dilated += ((d % step == 0) & (np.abs(d) <= radius * step)).astype(np.float32)
    return local, dilated


def _head_mean_matrix():
    e = np.kron(np.eye(N_HEADS), np.full((HEAD_DIM, HEAD_DIM), 1.0 / HEAD_DIM))
    return e.astype(np.float32)


def _pad_lanes(v, width):
    return jnp.pad(v.reshape(1, -1), ((0, 0), (0, width - v.size)))


def kernel(x, norm1, w_in, qn_a, kn_a, conv_b, a_log_b, dt_bias_b, onorm_b, qn_c, kn_c, w_out, norm2, w_gate_up, w_down):
    batch, seq, d = x.shape
    depth = w_in.shape[0]
    t = batch * seq

    tab_a, tab_c = _rope_tables(seq)
    tab_a, tab_c = jnp.asarray(tab_a), jnp.asarray(tab_c)
    mult_local, mult_dil = _multiplicity_tables(seq)
    mult_local, mult_dil = jnp.asarray(mult_local, dtype=BF16), jnp.asarray(mult_dil, dtype=BF16)
    e_mat = jnp.asarray(_head_mean_matrix(), dtype=BF16)

    c0 = 3 * W_ATT
    c1 = c0 + 4 * W_GDN
    c2 = c1 + 4 * N_HEADS
    cq = [w_in[:, :, c2 + h * HEAD_DIM:c2 + (h + 1) * HEAD_DIM] for h in C_HEAD_ORDER]
    w_in_p = jnp.concatenate(
        [w_in[:, :, c0:c1], w_in[:, :, 0:c0]] + cq + [w_in[:, :, c2 + W_ATT:], w_in[:, :, c1:c2]], axis=-1)
    w_in_p = jnp.pad(w_in_p, ((0, 0), (0, 0), (0, IN_PAD - w_in_p.shape[-1]))).astype(BF16)

    w_out_a = w_out[:, 0:W_ATT].astype(BF16)
    w_out_b = w_out[:, W_ATT:W_ATT + W_GDN].astype(BF16)
    wc = w_out[:, W_ATT + W_GDN:]
    w_out_c = jnp.concatenate([wc[:, h * HEAD_DIM:(h + 1) * HEAD_DIM] for h in C_HEAD_ORDER], axis=1).astype(BF16)
    w_gu = w_gate_up.astype(BF16)
    w_dn = w_down.astype(BF16)

    tile4 = lambda v: jnp.tile(v.reshape(1, -1), (1, N_HEADS))

    xt = x.reshape(t, d)
    for i in range(depth):
        proj, proj_ab = _in_proj(xt, norm1[i].reshape(1, d), w_in_p[i])
        (qa_cls, ka_cls, va_cls), (qa, ka, va, qc, kc, vc) = _attn_prep(
            proj, tile4(qn_a[i]), tile4(kn_a[i]), tile4(qn_c[i]), tile4(kn_c[i]), tab_a, tab_c, e_mat, batch, seq)
        o1, lse1 = _attn_a_local(qa, ka, va, mult_local, batch, seq)
        o_a = _attn_a_dilated(qa_cls, ka_cls, va_cls, mult_dil, o1, lse1, batch, seq)
        o_c = _attn_c(qc, kc, vc, batch, seq)
        qkv_b = _gdn_prep(proj, conv_b[i], batch, seq)
        alog, dtb = _pad_lanes(a_log_b[i], 128), _pad_lanes(dt_bias_b[i], 128)
        o_f = _gdn(qkv_b, proj_ab, alog, dtb, batch, seq, rev=False)
        o_r = _gdn(qkv_b, proj_ab, alog, dtb, batch, seq, rev=True)
        xt = _out_proj(xt, o_a, o_f, o_r, proj, o_c, onorm_b[i].reshape(1, B_DK), w_out_a[i], w_out_b[i], w_out_c[i])
        xt = _ffn(xt, norm2[i].reshape(1, d), w_gu[i], w_dn[i])
    return xt.reshape(batch, seq, d)
```

```python
import functools

import numpy as np
import jax
import jax.numpy as jnp
from jax import lax
from jax.experimental import pallas as pl
from jax.experimental.pallas import tpu as pltpu

F32 = jnp.float32
BF16 = jnp.bfloat16

EPS = 1e-6
GRID_W = 64
HEAD_DIM = 64
N_HEADS = 4
A_PATTERNS = ((128, 1), (512, 4), (2048, 16))
A_ROT_HALF = 8
A_THETA = 500000.0
C_ROT_HALF = 16
C_THETA = 10000.0
B_DK = 128
B_CONV = 5
CHUNK = 64
NEG = -1e30

W_ATT = N_HEADS * HEAD_DIM
W_GDN = N_HEADS * B_DK
COL_BQ, COL_BK, COL_BV, COL_BZ = 0, 512, 1024, 1536
COL_AQ, COL_AK, COL_AV = 2048, 2304, 2560
COL_CQ, COL_CK, COL_CV = 2816, 3072, 3200
COL_AB = 3328
IN_PAD = 3456
N_CLS = 4
A_LOCAL_TQ = 128
C_HEAD_ORDER = (0, 2, 1, 3)

VMEM_LIMIT = 56 * 1024 * 1024


def _cparams(sem):
    return pltpu.CompilerParams(dimension_semantics=sem, vmem_limit_bytes=VMEM_LIMIT)


def _split2(x):
    hi = x.astype(BF16)
    lo = (x - hi.astype(F32)).astype(BF16)
    return hi, lo


def _dot(a, b):
    return jnp.dot(a, b, preferred_element_type=F32)


def _dot_nt(a, b):
    return lax.dot_general(a, b, (((1,), (1,)), ((), ())), preferred_element_type=F32)


def _dot_tn(a, b):
    return lax.dot_general(a, b, (((0,), (0,)), ((), ())), preferred_element_type=F32)


def _sigmoid(x):
    return 1.0 / (1.0 + jnp.exp(-x))


def _silu(x):
    hx = 0.5 * x
    return hx + hx * jnp.tanh(hx)


def _head_norm_rope(x, w, e_hi, cos, sin_lo, sin_hi, rot_half, scale):
    xx = x * x
    hi, lo = _split2(xx)
    ms = _dot(hi, e_hi) + _dot(lo, e_hi)
    y = (x * lax.rsqrt(ms + EPS)) * w
    width = x.shape[1]
    up = pltpu.roll(y, width - rot_half, 1)
    dn = pltpu.roll(y, rot_half, 1)
    out = y * cos + up * sin_lo + dn * sin_hi
    return out * scale


def _join_halves(halves):
    return jnp.concatenate(halves, axis=1)


def _inproj_kernel(x_ref, nw_ref, w_ref, qna_ref, kna_ref, qnc_ref, knc_ref, ta_ref, tc_ref, e_ref,
                   ob_ref, oab_ref, oqa_cls_ref, oka_cls_ref, ova_cls_ref, oqa_ref, oka_ref, ova_ref,
                   oqc_ref, okc_ref, ovc_ref, scr_ref):
    x = x_ref[...]
    ms = jnp.mean(x * x, axis=-1, keepdims=True)
    y = (x * lax.rsqrt(ms + EPS)) * nw_ref[...]
    res = _dot(y.astype(BF16), w_ref[...])
    ob_ref[...] = res[:, COL_BQ:COL_AQ].astype(BF16)
    oab_ref[...] = res[:, COL_AB:IN_PAD]

    e = e_ref[...]
    cos_a, sl_a, sh_a = ta_ref[0], ta_ref[1], ta_ref[2]
    cos_c, sl_c, sh_c = tc_ref[0], tc_ref[1], tc_ref[2]
    scale = HEAD_DIM ** -0.5
    rows = scr_ref.shape[1] // N_CLS

    def emit(v, tok_ref, cls_ref):
        tok_ref[...] = v.astype(BF16)
        for j in range(2):
            scr_ref[j] = v[:, j * 128:(j + 1) * 128]
        for c in range(N_CLS):
            cls_ref[c] = _join_halves([scr_ref[j, pl.ds(c, rows, stride=N_CLS), :] for j in range(2)]).astype(BF16)

    col = lambda c0, w: res[:, c0:c0 + w]
    emit(_head_norm_rope(col(COL_AQ, 256), qna_ref[...], e, cos_a, sl_a, sh_a, A_ROT_HALF, scale), oqa_ref, oqa_cls_ref)
    emit(_head_norm_rope(col(COL_AK, 256), kna_ref[...], e, cos_a, sl_a, sh_a, A_ROT_HALF, 1.0), oka_ref, oka_cls_ref)
    emit(col(COL_AV, 256), ova_ref, ova_cls_ref)
    oqc_ref[...] = _head_norm_rope(col(COL_CQ, 256), qnc_ref[...], e, cos_c, sl_c, sh_c, C_ROT_HALF, scale).astype(BF16)
    okc_ref[...] = _head_norm_rope(col(COL_CK, 128), knc_ref[...][:, :128], e[:128, :128],
                                   cos_c[:, :128], sl_c[:, :128], sh_c[:, :128], C_ROT_HALF, 1.0).astype(BF16)
    ovc_ref[...] = col(COL_CV, 128).astype(BF16)


def _in_proj(x, nw, w, qna, kna, qnc, knc, tab_a, tab_c, e_mat, batch, seq, tm=256):
    t, d = x.shape
    n = w.shape[1]
    ns = seq // tm
    vec = pl.BlockSpec((1, W_ATT), lambda i: (0, 0))
    tab = pl.BlockSpec((3, tm, W_ATT), lambda i: (0, i % ns, 0))
    ob = lambda wd: pl.BlockSpec((tm, wd), lambda i: (i, 0))
    cls = pl.BlockSpec((None, N_CLS, tm // N_CLS, W_ATT), lambda i: (i // ns, 0, i % ns, 0))
    widths = (256, 256, 256, 256, 128, 128)
    outs = pl.pallas_call(
        _inproj_kernel,
        grid=(t // tm,),
        in_specs=[pl.BlockSpec((tm, d), lambda i: (i, 0)),
                  pl.BlockSpec((1, d), lambda i: (0, 0)),
                  pl.BlockSpec((d, n), lambda i: (0, 0)),
                  vec, vec, vec, vec, tab, tab,
                  pl.BlockSpec((W_ATT, W_ATT), lambda i: (0, 0))],
        out_specs=[ob(COL_AQ), ob(n - COL_AB), cls, cls, cls] + [ob(wd) for wd in widths],
        out_shape=[jax.ShapeDtypeStruct((t, COL_AQ), BF16), jax.ShapeDtypeStruct((t, n - COL_AB), F32)]
                  + [jax.ShapeDtypeStruct((batch, N_CLS, seq // N_CLS, W_ATT), BF16)] * 3
                  + [jax.ShapeDtypeStruct((t, wd), BF16) for wd in widths],
        scratch_shapes=[pltpu.VMEM((2, tm, 128), F32)],
        compiler_params=_cparams(("parallel",)),
        name="in_proj",
    )(x, nw, w, qna, kna, qnc, knc, tab_a, tab_c, e_mat)
    return outs[0], outs[1], outs[2:5], outs[5:]


def _head_masks(width):
    lane = lax.broadcasted_iota(jnp.int32, (1, width), 1)
    return [(lane >= h * HEAD_DIM) & (lane < (h + 1) * HEAD_DIM) for h in range(width // HEAD_DIM)]


def _attn_a_local_kernel(q_ref, k_ref, v_ref, m_ref, o1_ref, lse_ref):
    i = pl.program_id(1)
    seq = k_ref.shape[0]
    tq, win = m_ref.shape[1], m_ref.shape[2]
    nblk = q_ref.shape[0] // tq
    last = seq // tq - 1
    hms = _head_masks(W_ATT)
    vs, scores = [], []
    for j in range(nblk):
        blk = i * nblk + j
        start = jnp.clip(blk * tq - (win - tq) // 2, 0, seq - win)
        rows = pl.ds(pl.multiple_of(start, 64), win)
        k = k_ref[rows, :]
        vs.append(v_ref[rows, :])
        mult = m_ref[1]
        if j == 0:
            mult = jnp.where(blk == 0, m_ref[0], mult)
        if j == nblk - 1:
            mult = jnp.where(blk == last, m_ref[2], mult)
        q = q_ref[j * tq:(j + 1) * tq, :]
        scores.append([jnp.where(mult > 0, _dot_nt(jnp.where(hm, q, jnp.zeros_like(q)), k), NEG) for hm in hms])
    maxes = [[jnp.max(s, axis=-1, keepdims=True) for s in row] for row in scores]
    probs = [[jnp.exp(s - m) for s, m in zip(srow, mrow)] for srow, mrow in zip(scores, maxes)]
    sums = [[jnp.sum(p, axis=-1, keepdims=True) for p in row] for row in probs]
    pvs = [[_dot(p.astype(BF16), v) for p in row] for row, v in zip(probs, vs)]
    for j in range(nblk):
        out = jnp.zeros((tq, W_ATT), F32)
        lse = jnp.zeros((tq, W_ATT), F32)
        for h in range(N_HEADS):
            out = jnp.where(hms[h], pvs[j][h] * (1.0 / sums[j][h]), out)
            lse = jnp.where(hms[h], maxes[j][h] + jnp.log(sums[j][h]), lse)
        for half in range(2):
            o1_ref[half, j * tq:(j + 1) * tq, :] = out[:, half * 128:(half + 1) * 128]
            lse_ref[half, j * tq:(j + 1) * tq, :] = lse[:, half * 128:(half + 1) * 128]


def _attn_a_local(q, k, v, mult, batch, seq, nblk=4):
    tq = mult.shape[1]
    rows = nblk * tq
    nq = seq // rows
    return pl.pallas_call(
        _attn_a_local_kernel,
        grid=(batch, nq),
        in_specs=[pl.BlockSpec((rows, W_ATT), lambda b, i: (b * nq + i, 0)),
                  pl.BlockSpec((seq, W_ATT), lambda b, i: (b, 0)),
                  pl.BlockSpec((seq, W_ATT), lambda b, i: (b, 0)),
                  pl.BlockSpec(mult.shape, lambda b, i: (0, 0, 0))],
        out_specs=[pl.BlockSpec((2, rows, 128), lambda b, i: (0, b * nq + i, 0)),
                   pl.BlockSpec((2, rows, 128), lambda b, i: (0, b * nq + i, 0))],
        out_shape=[jax.ShapeDtypeStruct((2, batch * seq, 128), F32), jax.ShapeDtypeStruct((2, batch * seq, 128), F32)],
        compiler_params=_cparams(("parallel", "parallel")),
        name="attn_a_local",
    )(q, k, v, mult)


def _attn_a_dilated_kernel(q_ref, k_ref, v_ref, m_ref, o1_ref, lse1_ref, o_ref):
    tq = q_ref.shape[1]
    mult = m_ref[...]
    multf = mult.astype(F32)
    hms = _head_masks(W_ATT)
    qs = [q_ref[c] for c in range(N_CLS)]
    scores = [[jnp.where(mult > 0, _dot_nt(jnp.where(hm, q, jnp.zeros_like(q)), k_ref[c]), NEG) for hm in hms]
              for c, q in enumerate(qs)]
    m2 = [[jnp.max(s, axis=-1, keepdims=True) for s in row] for row in scores]
    probs = [[jnp.exp(s - m) * multf for s, m in zip(srow, mrow)] for srow, mrow in zip(scores, m2)]
    l2 = [[jnp.sum(p, axis=-1, keepdims=True) for p in row] for row in probs]
    pv2 = [[_dot(p.astype(BF16), v_ref[c]) for p in row] for c, row in enumerate(probs)]
    for c in range(N_CLS):
        m2b = jnp.zeros((tq, W_ATT), F32)
        l2b = jnp.zeros((tq, W_ATT), F32)
        pv2b = jnp.zeros((tq, W_ATT), F32)
        for h in range(N_HEADS):
            m2b = jnp.where(hms[h], m2[c][h], m2b)
            l2b = jnp.where(hms[h], l2[c][h], l2b)
            pv2b = jnp.where(hms[h], pv2[c][h], pv2b)
        tok = pl.ds(c, tq, stride=N_CLS)
        lse1 = _join_halves([lse1_ref[half, tok, :] for half in range(2)])
        o1 = _join_halves([o1_ref[half, tok, :] for half in range(2)])
        m = jnp.maximum(lse1, m2b)
        a1 = jnp.exp(lse1 - m)
        a2 = jnp.exp(m2b - m)
        out = (o1 * a1 + pv2b * a2) / (a1 + l2b * a2)
        for half in range(2):
            o_ref[half, tok, :] = out[:, half * 128:(half + 1) * 128]


def _attn_a_dilated(q, k, v, mult, o1, lse1, batch, seq, tq=128):
    rows = seq // N_CLS
    nu = rows // tq
    tok = pl.BlockSpec((2, N_CLS * tq, 128), lambda u, b: (0, b * nu + u, 0))
    return pl.pallas_call(
        _attn_a_dilated_kernel,
        grid=(nu, batch),
        in_specs=[pl.BlockSpec((None, N_CLS, tq, W_ATT), lambda u, b: (b, 0, u, 0)),
                  pl.BlockSpec((None, N_CLS, rows, W_ATT), lambda u, b: (b, 0, 0, 0)),
                  pl.BlockSpec((None, N_CLS, rows, W_ATT), lambda u, b: (b, 0, 0, 0)),
                  pl.BlockSpec((tq, rows), lambda u, b: (u, 0)),
                  tok, tok],
        out_specs=tok,
        out_shape=jax.ShapeDtypeStruct((2, batch * seq, 128), F32),
        compiler_params=_cparams(("parallel", "parallel")),
        name="attn_a_dilated",
    )(q, k, v, mult, o1, lse1)


def _attn_c_kernel(q_ref, k_ref, v_ref, o_ref):
    k = k_ref[...]
    v = v_ref[...]
    hms = _head_masks(128)
    qs = [q_ref[:, pair * 128:(pair + 1) * 128] for pair in range(2)]
    scores = [[_dot_nt(jnp.where(hm, q, jnp.zeros_like(q)), k) for hm in hms] for q in qs]
    maxes = [[jnp.max(s, axis=-1, keepdims=True) for s in row] for row in scores]
    probs = [[jnp.exp(s - m) for s, m in zip(srow, mrow)] for srow, mrow in zip(scores, maxes)]
    sums = [[jnp.sum(p, axis=-1, keepdims=True) for p in row] for row in probs]
    pvs = [[_dot(p.astype(BF16), v) for p in row] for row in probs]
    for pair in range(2):
        acc = jnp.zeros(qs[pair].shape, F32)
        for half in range(2):
            acc = jnp.where(hms[half], pvs[pair][half] * (1.0 / sums[pair][half]), acc)
        o_ref[:, pair * 128:(pair + 1) * 128] = acc.astype(o_ref.dtype)


def _attn_c(q, k, v, batch, seq, tq=256):
    nq = seq // tq
    return pl.pallas_call(
        _attn_c_kernel,
        grid=(batch, nq),
        in_specs=[pl.BlockSpec((tq, W_ATT), lambda b, i: (b * nq + i, 0)),
                  pl.BlockSpec((seq, 128), lambda b, i: (b, 0)),
                  pl.BlockSpec((seq, 128), lambda b, i: (b, 0))],
        out_specs=pl.BlockSpec((tq, W_ATT), lambda b, i: (b * nq + i, 0)),
        out_shape=jax.ShapeDtypeStruct((batch * seq, W_ATT), BF16),
        compiler_params=_cparams(("parallel", "parallel")),
        name="attn_c",
    )(q, k, v)


def _gdn_prep_kernel(x_ref, w_ref, o_ref):
    j = pl.program_id(1)
    x = x_ref[...].astype(F32)
    w = w_ref[...]
    seq = x.shape[0]
    pos = lax.broadcasted_iota(jnp.int32, (seq, 1), 0)
    acc = x * w[B_CONV // 2:B_CONV // 2 + 1, :]
    for tap in range(B_CONV):
        off = tap - B_CONV // 2
        if off == 0:
            continue
        sh = pltpu.roll(x, (-off) % seq, 0)
        ok = (pos + off >= 0) & (pos + off < seq)
        acc = acc + jnp.where(ok, sh, 0.0) * w[tap:tap + 1, :]
    y = _silu(acc)
    norm_scale = jnp.where(j == 0, B_DK ** -0.5, 1.0).astype(F32)
    for h in range(N_HEADS):
        yh = y[:, h * B_DK:(h + 1) * B_DK]
        ss = jnp.sum(yh * yh, axis=-1, keepdims=True)
        nh = yh * (lax.rsqrt(ss + EPS) * norm_scale)
        o_ref[:, h * B_DK:(h + 1) * B_DK] = jnp.where(j < 2, nh, yh).astype(o_ref.dtype)


def _gdn_prep(proj, conv_w, batch, seq):
    return pl.pallas_call(
        _gdn_prep_kernel,
        grid=(batch, 3),
        in_specs=[pl.BlockSpec((seq, W_GDN), lambda b, j: (b, j)),
                  pl.BlockSpec((B_CONV, W_GDN), lambda b, j: (0, j))],
        out_specs=pl.BlockSpec((seq, W_GDN), lambda b, j: (b, j)),
        out_shape=jax.ShapeDtypeStruct((batch * seq, 3 * W_GDN), BF16),
        compiler_params=_cparams(("parallel", "parallel")),
        name="gdn_prep",
    )(proj, conv_w)


def _lane_block(width, block, h):
    lane = lax.broadcasted_iota(jnp.int32, (1, width), 1)
    return (lane >= h * block) & (lane < (h + 1) * block)


def _block_diag(x, block):
    width = x.shape[1]
    zero = jnp.zeros_like(x)
    return jnp.concatenate([jnp.where(_lane_block(width, block, h), x, zero) for h in range(N_HEADS)], axis=0)


def _unit_tri_inverse_cat(a_list, eye_t):
    p = [eye_t - a for a in a_list]
    ab = [a.astype(BF16) for a in a_list]
    ap = [_dot(x, _block_diag(x, CHUNK)) for x in ab]
    for lvl in range(5):
        apb = [x.astype(BF16) for x in ap]
        bd = [_block_diag(x, CHUNK) for x in apb]
        if lvl < 4:
            r = [_dot(jnp.concatenate([pi.astype(BF16), xi], axis=0), bi) for pi, xi, bi in zip(p, apb, bd)]
            p = [pi + ri[0:CHUNK] for pi, ri in zip(p, r)]
            ap = [ri[CHUNK:2 * CHUNK] for ri in r]
        else:
            p = [pi + _dot(pi.astype(BF16), bi) for pi, bi in zip(p, bd)]
    return p


def _chunk_cumsum(g, rev):
    ridx = lax.broadcasted_iota(jnp.int32, g.shape, 0)
    x = g
    s = 1
    while s < CHUNK:
        if rev:
            sh = jnp.where(ridx < CHUNK - s, pltpu.roll(x, CHUNK - s, 0), 0.0)
        else:
            sh = jnp.where(ridx >= s, pltpu.roll(x, s, 0), 0.0)
        x = x + sh
        s *= 2
    return x


def _gdn_chunk_operands(qkv_ref, ab_ref, alog, dtb, rows, rev):
    c0 = N_HEADS if rev else 0
    ab = ab_ref[rows, :]
    xg = ab + dtb
    softplus = jnp.maximum(xg, 0.0) + jnp.log(1.0 + jnp.exp(-jnp.abs(xg)))
    g = -jnp.exp(alog) * softplus
    beta = _sigmoid(ab)
    gc = _chunk_cumsum(g, rev)
    g_last = gc[0:1, :] if rev else gc[CHUNK - 1:CHUNK, :]
    eg = jnp.exp(gc)
    e_rest = jnp.exp(g_last - gc)

    wcat = N_HEADS * CHUNK
    ii = lax.broadcasted_iota(jnp.int32, (CHUNK, 1), 0)
    jl = lax.broadcasted_iota(jnp.int32, (1, wcat), 1) & (CHUNK - 1)
    eye_b = ii == jl
    incl = (ii <= jl) if rev else (ii >= jl)
    g_col = jnp.zeros((CHUNK, wcat), F32)
    for h in range(N_HEADS):
        g_col = jnp.where(_lane_block(wcat, CHUNK, h), gc[:, c0 + h:c0 + h + 1], g_col)
    g_row = jnp.sum(jnp.where(eye_b, g_col, 0.0), axis=0, keepdims=True)
    dec = jnp.exp(jnp.where(incl, g_col - g_row, NEG))

    kb, vb, kbg, qg, kd = [], [], [], [], []
    for h in range(N_HEADS):
        c = c0 + h
        qh = qkv_ref[rows, h * B_DK:(h + 1) * B_DK].astype(F32)
        kh = qkv_ref[rows, W_GDN + h * B_DK:W_GDN + (h + 1) * B_DK].astype(F32)
        vh = qkv_ref[rows, 2 * W_GDN + h * B_DK:2 * W_GDN + (h + 1) * B_DK].astype(F32)
        bcol = beta[:, 8 + c:9 + c]
        egc = eg[:, c:c + 1]
        kbh = kh * bcol
        kb.append(kbh)
        vb.append(vh * bcol)
        kbg.append(kbh * egc)
        qg.append(qh * egc)
        kd.append((kh * e_rest[:, c:c + 1]).astype(BF16))
    cat = lambda xs: jnp.concatenate(xs, axis=1)
    return dict(dec=dec, eye_b=eye_b, kb=cat(kb).astype(BF16), vb=cat(vb).astype(BF16), kbg=cat(kbg).astype(BF16),
                qg=cat(qg), kd=kd, eg_last=jnp.exp(g_last),
                k=qkv_ref[rows, W_GDN:2 * W_GDN], q=qkv_ref[rows, 0:W_GDN])


def _gdn_solve_chunks(ops):
    kq = [_dot_nt(jnp.concatenate([o["kb"], o["q"]], axis=0), _block_diag(o["k"], B_DK)) for o in ops]
    a_cat = [jnp.where(o["eye_b"], 0.0, x[0:CHUNK] * o["dec"]) for o, x in zip(ops, kq)]
    qk = [(x[CHUNK:2 * CHUNK] * o["dec"]).astype(BF16) for o, x in zip(ops, kq)]
    eye_t = jnp.where(ops[0]["eye_b"], 1.0, 0.0).astype(F32)
    t_cat = [t.astype(BF16) for t in _unit_tri_inverse_cat(a_cat, eye_t)]
    ub = [_dot(t, _block_diag(o["vb"], B_DK)).astype(BF16) for o, t in zip(ops, t_cat)]
    wb = [_dot(t, _block_diag(o["kbg"], B_DK)).astype(BF16) for o, t in zip(ops, t_cat)]
    q_eff = [o["qg"] - _dot(x, _block_diag(w, B_DK)) for o, x, w in zip(ops, qk, wb)]
    o_intra = [_dot(x, _block_diag(u, B_DK)) for x, u in zip(qk, ub)]
    mc = []
    for o, u, w in zip(ops, ub, wb):
        per_head = []
        for h in range(N_HEADS):
            cols = slice(h * B_DK, (h + 1) * B_DK)
            per_head.append(_dot_tn(o["kd"][h], jnp.concatenate([w[:, cols], u[:, cols]], axis=1)))
        mc.append(per_head)
    return mc, q_eff, o_intra


def _gdn_kernel(qkv_ref, ab_ref, alog_ref, dtb_ref, o_ref, s_ref, m_ref, c_ref, qe_ref, oi_ref, egl_ref, *, rev, group):
    seq = qkv_ref.shape[0]
    nc = seq // CHUNK
    c0 = N_HEADS if rev else 0
    alog = alog_ref[...]
    dtb = dtb_ref[...]

    def solve(i, carry):
        ns = [i * group + j for j in range(group)]
        rows = [pl.ds(pl.multiple_of(n * CHUNK, CHUNK), CHUNK) for n in ns]
        ops = [_gdn_chunk_operands(qkv_ref, ab_ref, alog, dtb, r, rev) for r in rows]
        mc, q_eff, o_intra = _gdn_solve_chunks(ops)
        for j, n in enumerate(ns):
            for h in range(N_HEADS):
                m_ref[n * N_HEADS + h] = mc[j][h][:, 0:B_DK].astype(BF16)
                c_ref[n * N_HEADS + h] = mc[j][h][:, B_DK:2 * B_DK]
            qe_ref[rows[j], :] = q_eff[j].astype(BF16)
            oi_ref[rows[j], :] = o_intra[j]
            egl_ref[n] = jnp.broadcast_to(ops[j]["eg_last"], (8, 128))
        return carry

    lax.fori_loop(0, nc // group, solve, 0)

    s_ref[...] = jnp.zeros(s_ref.shape, F32)

    def scan(i, carry):
        n = (nc - 1 - i) if rev else i
        rows = pl.ds(pl.multiple_of(n * CHUNK, CHUNK), CHUNK)
        egl = egl_ref[n]
        new_s, new_o = [], []
        for h in range(N_HEADS):
            cols = slice(h * B_DK, (h + 1) * B_DK)
            state = s_ref[h]
            lhs = jnp.concatenate([m_ref[n * N_HEADS + h], qe_ref[rows, cols]], axis=0)
            r = _dot(lhs, state.astype(BF16))
            new_s.append(state * egl[0:1, c0 + h:c0 + h + 1] - r[0:B_DK] + c_ref[n * N_HEADS + h])
            new_o.append(oi_ref[rows, cols] + r[B_DK:B_DK + CHUNK])
        for h in range(N_HEADS):
            s_ref[h] = new_s[h]
            o_ref[rows, h * B_DK:(h + 1) * B_DK] = new_o[h].astype(o_ref.dtype)
        return carry

    lax.fori_loop(0, nc, scan, 0)


def _gdn(qkv, ab, alog, dtb, batch, seq, rev, group=8):
    nc = seq // CHUNK
    return pl.pallas_call(
        functools.partial(_gdn_kernel, rev=rev, group=group),
        grid=(batch,),
        in_specs=[pl.BlockSpec((seq, 3 * W_GDN), lambda b: (b, 0)),
                  pl.BlockSpec((seq, 128), lambda b: (b, 0)),
                  pl.BlockSpec((1, 128), lambda b: (0, 0)),
                  pl.BlockSpec((1, 128), lambda b: (0, 0))],
        out_specs=pl.BlockSpec((seq, W_GDN), lambda b: (b, 0)),
        out_shape=jax.ShapeDtypeStruct((batch * seq, W_GDN), BF16),
        scratch_shapes=[pltpu.VMEM((N_HEADS, B_DK, B_DK), F32),
                        pltpu.VMEM((nc * N_HEADS, B_DK, B_DK), BF16),
                        pltpu.VMEM((nc * N_HEADS, B_DK, B_DK), F32),
                        pltpu.VMEM((seq, W_GDN), BF16),
                        pltpu.VMEM((seq, W_GDN), F32),
                        pltpu.VMEM((nc, 8, 128), F32)],
        compiler_params=_cparams(("parallel",)),
        name="gdn_bwd" if rev else "gdn_fwd",
    )(qkv, ab, alog, dtb)


def _ffn_kernel(x_ref, oa_ref, of_ref, ob_ref, z_ref, oc_ref, onorm_ref, wa_ref, wb_ref, wc_ref,
                nw_ref, wg_ref, wu_ref, wd_ref, o_ref, xn_ref, acc_ref):
    j = pl.program_id(1)

    @pl.when(j == 0)
    def _():
        x = x_ref[...] + _dot(_join_halves([oa_ref[0], oa_ref[1]]).astype(BF16), wa_ref[...])
        x = x + _dot(oc_ref[...], wc_ref[...])
        for h in range(N_HEADS):
            cols = slice(h * B_DK, (h + 1) * B_DK)
            o = of_ref[:, cols].astype(F32) + ob_ref[:, cols].astype(F32)
            ms = jnp.mean(o * o, axis=-1, keepdims=True)
            y = (o * lax.rsqrt(ms + EPS)) * onorm_ref[...]
            gated = (y * _silu(z_ref[:, cols].astype(F32))).astype(BF16)
            x = x + _dot(gated, wb_ref[h * B_DK:(h + 1) * B_DK, :])
        ms = jnp.mean(x * x, axis=-1, keepdims=True)
        xn_ref[...] = ((x * lax.rsqrt(ms + EPS)) * nw_ref[...]).astype(BF16)
        acc_ref[...] = x

    xn = xn_ref[...]
    gate = _dot(xn, wg_ref[...])
    up = _dot(xn, wu_ref[...])
    hidden = _silu(gate) * up
    acc_ref[...] += _dot(hidden.astype(BF16), wd_ref[...])

    @pl.when(j == pl.num_programs(1) - 1)
    def _():
        o_ref[...] = acc_ref[...]


def _out_proj_ffn(x, oa, o_f, o_b, proj_b, oc, onorm, wa, wb, wc, nw, w_gate_up, w_down, tm=512, tf=1408):
    t, d = x.shape
    dff = w_down.shape[0]
    nf = dff // tf
    row = lambda w: pl.BlockSpec((tm, w), lambda i, j: (i, 0))
    full = lambda w: pl.BlockSpec((w, d), lambda i, j: (0, 0))
    return pl.pallas_call(
        _ffn_kernel,
        grid=(t // tm, nf),
        in_specs=[row(d), pl.BlockSpec((2, tm, 128), lambda i, j: (0, i, 0)), row(W_GDN), row(W_GDN),
                  pl.BlockSpec((tm, W_GDN), lambda i, j: (i, COL_BZ // W_GDN)),
                  row(W_ATT), pl.BlockSpec((1, B_DK), lambda i, j: (0, 0)),
                  full(W_ATT), full(W_GDN), full(W_ATT),
                  pl.BlockSpec((1, d), lambda i, j: (0, 0)),
                  pl.BlockSpec((d, tf), lambda i, j: (0, j)),
                  pl.BlockSpec((d, tf), lambda i, j: (0, j + nf)),
                  pl.BlockSpec((tf, d), lambda i, j: (j, 0))],
        out_specs=pl.BlockSpec((tm, d), lambda i, j: (i, 0)),
        out_shape=jax.ShapeDtypeStruct((t, d), F32),
        scratch_shapes=[pltpu.VMEM((tm, d), BF16), pltpu.VMEM((tm, d), F32)],
        compiler_params=_cparams(("parallel", "arbitrary")),
        name="ffn",
    )(x, oa, o_f, o_b, proj_b, oc, onorm, wa, wb, wc, nw, w_gate_up, w_gate_up, w_down)


def _rope_tables(seq):
    t = np.arange(seq, dtype=np.float64)

    def build(groups):
        cos = np.ones((seq, HEAD_DIM))
        s_lo = np.zeros((seq, HEAD_DIM))
        s_hi = np.zeros((seq, HEAD_DIM))
        for start, half, theta, pos in groups:
            inv = np.float64(np.float32(theta)) ** (-np.arange(half, dtype=np.float64) / half)
            ang = pos[:, None] * inv[None, :]
            cos[:, start:start + half] = np.cos(ang)
            cos[:, start + half:start + 2 * half] = np.cos(ang)
            s_lo[:, start:start + half] = -np.sin(ang)
            s_hi[:, start + half:start + 2 * half] = np.sin(ang)
        tab = np.stack([cos, s_lo, s_hi])
        return np.tile(tab, (1, 1, N_HEADS)).astype(np.float32)

    tab_a = build([(0, A_ROT_HALF, A_THETA, t)])
    row = np.floor(t / GRID_W)
    col = t - row * GRID_W
    tab_c = build([(0, C_ROT_HALF, C_THETA, row), (2 * C_ROT_HALF, C_ROT_HALF, C_THETA, col)])
    return tab_a, tab_c


def _multiplicity_tables(seq):
    (w0, d0), rest = A_PATTERNS[0], A_PATTERNS[1:]
    assert d0 == 1 and all(d % N_CLS == 0 for _, d in rest)
    r0 = w0 // 2
    tq, win = A_LOCAL_TQ, A_LOCAL_TQ + 2 * r0
    qi = np.arange(tq)[:, None]
    kj = np.arange(win)[None, :]
    local = np.stack([(np.abs(qi + off - kj) <= r0) for off in (0, r0, 2 * r0)]).astype(np.float32)
    n = seq // N_CLS
    d = np.arange(n)[:, None] - np.arange(n)[None, :]
    dilated = np.zeros((n, n), np.float32)
    for window, dil in rest:
        step = dil // N_CLS
        radius = window // (2 * dil)
        dilated += ((d % step == 0) & (np.abs(d) <= radius * step)).astype(np.float32)
    return local, dilated


def _head_mean_matrix():
    e = np.kron(np.eye(N_HEADS), np.full((HEAD_DIM, HEAD_DIM), 1.0 / HEAD_DIM))
    return e.astype(np.float32)


def _pad_lanes(v, width):
    return jnp.pad(v.reshape(1, -1), ((0, 0), (0, width - v.size)))


def kernel(x, norm1, w_in, qn_a, kn_a, conv_b, a_log_b, dt_bias_b, onorm_b, qn_c, kn_c, w_out, norm2, w_gate_up, w_down):
    batch, seq, d = x.shape
    depth = w_in.shape[0]
    t = batch * seq

    tab_a, tab_c = _rope_tables(seq)
    tab_a, tab_c = jnp.asarray(tab_a), jnp.asarray(tab_c)
    mult_local, mult_dil = _multiplicity_tables(seq)
    mult_local, mult_dil = jnp.asarray(mult_local, dtype=BF16), jnp.asarray(mult_dil, dtype=BF16)
    e_mat = jnp.asarray(_head_mean_matrix(), dtype=BF16)

    c0 = 3 * W_ATT
    c1 = c0 + 4 * W_GDN
    c2 = c1 + 4 * N_HEADS
    cq = [w_in[:, :, c2 + h * HEAD_DIM:c2 + (h + 1) * HEAD_DIM] for h in C_HEAD_ORDER]
    w_in_p = jnp.concatenate(
        [w_in[:, :, c0:c1], w_in[:, :, 0:c0]] + cq + [w_in[:, :, c2 + W_ATT:], w_in[:, :, c1:c2]], axis=-1)
    w_in_p = jnp.pad(w_in_p, ((0, 0), (0, 0), (0, IN_PAD - w_in_p.shape[-1]))).astype(BF16)

    w_out_a = w_out[:, 0:W_ATT].astype(BF16)
    w_out_b = w_out[:, W_ATT:W_ATT + W_GDN].astype(BF16)
    wc = w_out[:, W_ATT + W_GDN:]
    w_out_c = jnp.concatenate([wc[:, h * HEAD_DIM:(h + 1) * HEAD_DIM] for h in C_HEAD_ORDER], axis=1).astype(BF16)
    w_gu = w_gate_up.astype(BF16)
    w_dn = w_down.astype(BF16)

    tile4 = lambda v: jnp.tile(v.reshape(1, -1), (1, N_HEADS))

    xt = x.reshape(t, d)
    for i in range(depth):
        proj, proj_ab, (qa_cls, ka_cls, va_cls), (qa, ka, va, qc, kc, vc) = _in_proj(
            xt, norm1[i].reshape(1, d), w_in_p[i], tile4(qn_a[i]), tile4(kn_a[i]), tile4(qn_c[i]), tile4(kn_c[i]),
            tab_a, tab_c, e_mat, batch, seq)
        o1, lse1 = _attn_a_local(qa, ka, va, mult_local, batch, seq)
        o_a = _attn_a_dilated(qa_cls, ka_cls, va_cls, mult_dil, o1, lse1, batch, seq)
        o_c = _attn_c(qc, kc, vc, batch, seq)
        qkv_b = _gdn_prep(proj, conv_b[i], batch, seq)
        alog, dtb = _pad_lanes(a_log_b[i], 128), _pad_lanes(dt_bias_b[i], 128)
        o_f = _gdn(qkv_b, proj_ab, alog, dtb, batch, seq, rev=False)
        o_r = _gdn(qkv_b, proj_ab, alog, dtb, batch, seq, rev=True)
        xt = _out_proj_ffn(xt, o_a, o_f, o_r, proj, o_c, onorm_b[i].reshape(1, B_DK),
                           w_out_a[i], w_out_b[i], w_out_c[i], norm2[i].reshape(1, d), w_gu[i], w_dn[i])
    return xt.reshape(batch, seq, d)
```

```python
import functools

import numpy as np
import jax
import jax.numpy as jnp
from jax import lax
from jax.experimental import pallas as pl
from jax.experimental.pallas import tpu as pltpu

F32 = jnp.float32
BF16 = jnp.bfloat16

EPS = 1e-6
GRID_W = 64
HEAD_DIM = 64
N_HEADS = 4
A_PATTERNS = ((128, 1), (512, 4), (2048, 16))
A_ROT_HALF = 8
A_THETA = 500000.0
C_ROT_HALF = 16
C_THETA = 10000.0
B_DK = 128
B_CONV = 5
CHUNK = 64
NEG = -1e30

W_ATT = N_HEADS * HEAD_DIM
W_GDN = N_HEADS * B_DK
COL_BQ, COL_BK, COL_BV, COL_BZ = 0, 512, 1024, 1536
COL_AQ, COL_AK, COL_AV = 2048, 2304, 2560
COL_CQ, COL_CK, COL_CV = 2816, 3072, 3200
COL_AB = 3328
IN_PAD = 3456
N_CLS = 4
A_LOCAL_TQ = 128
C_HEAD_ORDER = (0, 2, 1, 3)

VMEM_LIMIT = 56 * 1024 * 1024


def _cparams(sem):
    return pltpu.CompilerParams(dimension_semantics=sem, vmem_limit_bytes=VMEM_LIMIT)


def _split2(x):
    hi = x.astype(BF16)
    lo = (x - hi.astype(F32)).astype(BF16)
    return hi, lo


def _dot(a, b):
    return jnp.dot(a, b, preferred_element_type=F32)


def _dot_nt(a, b):
    return lax.dot_general(a, b, (((1,), (1,)), ((), ())), preferred_element_type=F32)


def _dot_tn(a, b):
    return lax.dot_general(a, b, (((0,), (0,)), ((), ())), preferred_element_type=F32)


def _sigmoid(x):
    return 1.0 / (1.0 + jnp.exp(-x))


def _silu(x):
    hx = 0.5 * x
    return hx + hx * jnp.tanh(hx)


def _head_norm_rope(x, w, e_hi, cos, sin_lo, sin_hi, rot_half, scale):
    xx = x * x
    hi, lo = _split2(xx)
    ms = _dot(hi, e_hi) + _dot(lo, e_hi)
    y = (x * lax.rsqrt(ms + EPS)) * w
    width = x.shape[1]
    up = pltpu.roll(y, width - rot_half, 1)
    dn = pltpu.roll(y, rot_half, 1)
    out = y * cos + up * sin_lo + dn * sin_hi
    return out * scale


def _join_halves(halves):
    return jnp.concatenate(halves, axis=1)


def _inproj_kernel(x_ref, nw_ref, w_ref, qna_ref, kna_ref, qnc_ref, knc_ref, ta_ref, tc_ref, e_ref,
                   ob_ref, oab_ref, oqa_cls_ref, oka_cls_ref, ova_cls_ref, oqa_ref, oka_ref, ova_ref,
                   oqc_ref, okc_ref, ovc_ref, scr_ref):
    tm = x_ref.shape[0]
    nh = 2
    hr = tm // nh
    rsl = [slice(h * hr, (h + 1) * hr) for h in range(nh)]
    ys = []
    for r in rsl:
        x = x_ref[r, :]
        ms = jnp.mean(x * x, axis=-1, keepdims=True)
        ys.append(((x * lax.rsqrt(ms + EPS)) * nw_ref[...]).astype(BF16))
    ress = [_dot(y, w_ref[...]) for y in ys]

    e = e_ref[...]
    scale = HEAD_DIM ** -0.5
    crows = hr // N_CLS
    for h, (r, res) in enumerate(zip(rsl, ress)):
        ob_ref[r, :] = res[:, COL_BQ:COL_AQ].astype(BF16)
        oab_ref[r, :] = res[:, COL_AB:IN_PAD]
        cos_a, sl_a, sh_a = ta_ref[0, r, :], ta_ref[1, r, :], ta_ref[2, r, :]
        cos_c, sl_c, sh_c = tc_ref[0, r, :], tc_ref[1, r, :], tc_ref[2, r, :]

        def emit(v, tok_ref, cls_ref):
            tok_ref[r, :] = v.astype(BF16)
            for j in range(2):
                scr_ref[h, j] = v[:, j * 128:(j + 1) * 128]
            for c in range(N_CLS):
                grouped = _join_halves([scr_ref[h, j, pl.ds(c, crows, stride=N_CLS), :] for j in range(2)])
                cls_ref[c, h * crows:(h + 1) * crows, :] = grouped.astype(BF16)

        col = lambda c0, w: res[:, c0:c0 + w]
        emit(_head_norm_rope(col(COL_AQ, 256), qna_ref[...], e, cos_a, sl_a, sh_a, A_ROT_HALF, scale), oqa_ref, oqa_cls_ref)
        emit(_head_norm_rope(col(COL_AK, 256), kna_ref[...], e, cos_a, sl_a, sh_a, A_ROT_HALF, 1.0), oka_ref, oka_cls_ref)
        emit(col(COL_AV, 256), ova_ref, ova_cls_ref)
        oqc_ref[r, :] = _head_norm_rope(col(COL_CQ, 256), qnc_ref[...], e, cos_c, sl_c, sh_c, C_ROT_HALF, scale).astype(BF16)
        okc_ref[r, :] = _head_norm_rope(col(COL_CK, 128), knc_ref[...][:, :128], e[:128, :128],
                                        cos_c[:, :128], sl_c[:, :128], sh_c[:, :128], C_ROT_HALF, 1.0).astype(BF16)
        ovc_ref[r, :] = col(COL_CV, 128).astype(BF16)


def _in_proj(x, nw, w, qna, kna, qnc, knc, tab_a, tab_c, e_mat, batch, seq, tm=512):
    t, d = x.shape
    n = w.shape[1]
    ns = seq // tm
    vec = pl.BlockSpec((1, W_ATT), lambda i: (0, 0))
    tab = pl.BlockSpec((3, tm, W_ATT), lambda i: (0, i % ns, 0))
    ob = lambda wd: pl.BlockSpec((tm, wd), lambda i: (i, 0))
    cls = pl.BlockSpec((None, N_CLS, tm // N_CLS, W_ATT), lambda i: (i // ns, 0, i % ns, 0))
    widths = (256, 256, 256, 256, 128, 128)
    outs = pl.pallas_call(
        _inproj_kernel,
        grid=(t // tm,),
        in_specs=[pl.BlockSpec((tm, d), lambda i: (i, 0)),
                  pl.BlockSpec((1, d), lambda i: (0, 0)),
                  pl.BlockSpec((d, n), lambda i: (0, 0)),
                  vec, vec, vec, vec, tab, tab,
                  pl.BlockSpec((W_ATT, W_ATT), lambda i: (0, 0))],
        out_specs=[ob(COL_AQ), ob(n - COL_AB), cls, cls, cls] + [ob(wd) for wd in widths],
        out_shape=[jax.ShapeDtypeStruct((t, COL_AQ), BF16), jax.ShapeDtypeStruct((t, n - COL_AB), F32)]
                  + [jax.ShapeDtypeStruct((batch, N_CLS, seq // N_CLS, W_ATT), BF16)] * 3
                  + [jax.ShapeDtypeStruct((t, wd), BF16) for wd in widths],
        scratch_shapes=[pltpu.VMEM((2, 2, tm // 2, 128), F32)],
        compiler_params=_cparams(("parallel",)),
        name="in_proj",
    )(x, nw, w, qna, kna, qnc, knc, tab_a, tab_c, e_mat)
    return outs[0], outs[1], outs[2:5], outs[5:]


def _head_masks(width):
    lane = lax.broadcasted_iota(jnp.int32, (1, width), 1)
    return [(lane >= h * HEAD_DIM) & (lane < (h + 1) * HEAD_DIM) for h in range(width // HEAD_DIM)]


def _attn_a_local_kernel(q_ref, k_ref, v_ref, m_ref, o1_ref, lse_ref):
    i = pl.program_id(1)
    seq = k_ref.shape[0]
    tq, win = m_ref.shape[1], m_ref.shape[2]
    nblk = q_ref.shape[0] // tq
    last = seq // tq - 1
    hms = _head_masks(W_ATT)
    vs, scores = [], []
    for j in range(nblk):
        blk = i * nblk + j
        start = jnp.clip(blk * tq - (win - tq) // 2, 0, seq - win)
        rows = pl.ds(pl.multiple_of(start, 64), win)
        k = k_ref[rows, :]
        vs.append(v_ref[rows, :])
        mult = m_ref[1]
        if j == 0:
            mult = jnp.where(blk == 0, m_ref[0], mult)
        if j == nblk - 1:
            mult = jnp.where(blk == last, m_ref[2], mult)
        q = q_ref[j * tq:(j + 1) * tq, :]
        scores.append([jnp.where(mult > 0, _dot_nt(jnp.where(hm, q, jnp.zeros_like(q)), k), NEG) for hm in hms])
    maxes = [[jnp.max(s, axis=-1, keepdims=True) for s in row] for row in scores]
    probs = [[jnp.exp(s - m) for s, m in zip(srow, mrow)] for srow, mrow in zip(scores, maxes)]
    sums = [[jnp.sum(p, axis=-1, keepdims=True) for p in row] for row in probs]
    pvs = [[_dot(p.astype(BF16), v) for p in row] for row, v in zip(probs, vs)]
    for j in range(nblk):
        out = jnp.zeros((tq, W_ATT), F32)
        lse = jnp.zeros((tq, W_ATT), F32)
        for h in range(N_HEADS):
            out = jnp.where(hms[h], pvs[j][h] * (1.0 / sums[j][h]), out)
            lse = jnp.where(hms[h], maxes[j][h] + jnp.log(sums[j][h]), lse)
        for half in range(2):
            o1_ref[half, j * tq:(j + 1) * tq, :] = out[:, half * 128:(half + 1) * 128]
            lse_ref[half, j * tq:(j + 1) * tq, :] = lse[:, half * 128:(half + 1) * 128]


def _attn_a_local(q, k, v, mult, batch, seq, nblk=4):
    tq = mult.shape[1]
    rows = nblk * tq
    nq = seq // rows
    return pl.pallas_call(
        _attn_a_local_kernel,
        grid=(batch, nq),
        in_specs=[pl.BlockSpec((rows, W_ATT), lambda b, i: (b * nq + i, 0)),
                  pl.BlockSpec((seq, W_ATT), lambda b, i: (b, 0)),
                  pl.BlockSpec((seq, W_ATT), lambda b, i: (b, 0)),
                  pl.BlockSpec(mult.shape, lambda b, i: (0, 0, 0))],
        out_specs=[pl.BlockSpec((2, rows, 128), lambda b, i: (0, b * nq + i, 0)),
                   pl.BlockSpec((2, rows, 128), lambda b, i: (0, b * nq + i, 0))],
        out_shape=[jax.ShapeDtypeStruct((2, batch * seq, 128), F32), jax.ShapeDtypeStruct((2, batch * seq, 128), F32)],
        compiler_params=_cparams(("parallel", "parallel")),
        name="attn_a_local",
    )(q, k, v, mult)


def _attn_a_dilated_kernel(q_ref, k_ref, v_ref, m_ref, o1_ref, lse1_ref, o_ref):
    tq = q_ref.shape[1]
    mult = m_ref[...]
    multf = mult.astype(F32)
    hms = _head_masks(W_ATT)
    qs = [q_ref[c] for c in range(N_CLS)]
    scores = [[jnp.where(mult > 0, _dot_nt(jnp.where(hm, q, jnp.zeros_like(q)), k_ref[c]), NEG) for hm in hms]
              for c, q in enumerate(qs)]
    m2 = [[jnp.max(s, axis=-1, keepdims=True) for s in row] for row in scores]
    probs = [[jnp.exp(s - m) * multf for s, m in zip(srow, mrow)] for srow, mrow in zip(scores, m2)]
    l2 = [[jnp.sum(p, axis=-1, keepdims=True) for p in row] for row in probs]
    pv2 = [[_dot(p.astype(BF16), v_ref[c]) for p in row] for c, row in enumerate(probs)]
    for c in range(N_CLS):
        m2b = jnp.zeros((tq, W_ATT), F32)
        l2b = jnp.zeros((tq, W_ATT), F32)
        pv2b = jnp.zeros((tq, W_ATT), F32)
        for h in range(N_HEADS):
            m2b = jnp.where(hms[h], m2[c][h], m2b)
            l2b = jnp.where(hms[h], l2[c][h], l2b)
            pv2b = jnp.where(hms[h], pv2[c][h], pv2b)
        tok = pl.ds(c, tq, stride=N_CLS)
        lse1 = _join_halves([lse1_ref[half, tok, :] for half in range(2)])
        o1 = _join_halves([o1_ref[half, tok, :] for half in range(2)])
        m = jnp.maximum(lse1, m2b)
        a1 = jnp.exp(lse1 - m)
        a2 = jnp.exp(m2b - m)
        out = (o1 * a1 + pv2b * a2) / (a1 + l2b * a2)
        for half in range(2):
            o_ref[half, tok, :] = out[:, half * 128:(half + 1) * 128]


def _attn_a_dilated(q, k, v, mult, o1, lse1, batch, seq, tq=128):
    rows = seq // N_CLS
    nu = rows // tq
    tok = pl.BlockSpec((2, N_CLS * tq, 128), lambda u, b: (0, b * nu + u, 0))
    return pl.pallas_call(
        _attn_a_dilated_kernel,
        grid=(nu, batch),
        in_specs=[pl.BlockSpec((None, N_CLS, tq, W_ATT), lambda u, b: (b, 0, u, 0)),
                  pl.BlockSpec((None, N_CLS, rows, W_ATT), lambda u, b: (b, 0, 0, 0)),
                  pl.BlockSpec((None, N_CLS, rows, W_ATT), lambda u, b: (b, 0, 0, 0)),
                  pl.BlockSpec((tq, rows), lambda u, b: (u, 0)),
                  tok, tok],
        out_specs=tok,
        out_shape=jax.ShapeDtypeStruct((2, batch * seq, 128), F32),
        compiler_params=_cparams(("parallel", "parallel")),
        name="attn_a_dilated",
    )(q, k, v, mult, o1, lse1)


def _attn_c_kernel(q_ref, k_ref, v_ref, o_ref):
    k = k_ref[...]
    v = v_ref[...]
    hms = _head_masks(128)
    qs = [q_ref[:, pair * 128:(pair + 1) * 128] for pair in range(2)]
    scores = [[_dot_nt(jnp.where(hm, q, jnp.zeros_like(q)), k) for hm in hms] for q in qs]
    maxes = [[jnp.max(s, axis=-1, keepdims=True) for s in row] for row in scores]
    probs = [[jnp.exp(s - m) for s, m in zip(srow, mrow)] for srow, mrow in zip(scores, maxes)]
    sums = [[jnp.sum(p, axis=-1, keepdims=True) for p in row] for row in probs]
    pvs = [[_dot(p.astype(BF16), v) for p in row] for row in probs]
    for pair in range(2):
        acc = jnp.zeros(qs[pair].shape, F32)
        for half in range(2):
            acc = jnp.where(hms[half], pvs[pair][half] * (1.0 / sums[pair][half]), acc)
        o_ref[:, pair * 128:(pair + 1) * 128] = acc.astype(o_ref.dtype)


def _attn_c(q, k, v, batch, seq, tq=512):
    nq = seq // tq
    return pl.pallas_call(
        _attn_c_kernel,
        grid=(batch, nq),
        in_specs=[pl.BlockSpec((tq, W_ATT), lambda b, i: (b * nq + i, 0)),
                  pl.BlockSpec((seq, 128), lambda b, i: (b, 0)),
                  pl.BlockSpec((seq, 128), lambda b, i: (b, 0))],
        out_specs=pl.BlockSpec((tq, W_ATT), lambda b, i: (b * nq + i, 0)),
        out_shape=jax.ShapeDtypeStruct((batch * seq, W_ATT), BF16),
        compiler_params=_cparams(("parallel", "parallel")),
        name="attn_c",
    )(q, k, v)


def _gdn_prep_kernel(x_ref, w_ref, o_ref):
    j = pl.program_id(1)
    x = x_ref[...].astype(F32)
    w = w_ref[...]
    seq = x.shape[0]
    pos = lax.broadcasted_iota(jnp.int32, (seq, 1), 0)
    acc = x * w[B_CONV // 2:B_CONV // 2 + 1, :]
    for tap in range(B_CONV):
        off = tap - B_CONV // 2
        if off == 0:
            continue
        sh = pltpu.roll(x, (-off) % seq, 0)
        ok = (pos + off >= 0) & (pos + off < seq)
        acc = acc + jnp.where(ok, sh, 0.0) * w[tap:tap + 1, :]
    y = _silu(acc)
    norm_scale = jnp.where(j == 0, B_DK ** -0.5, 1.0).astype(F32)
    for h in range(N_HEADS):
        yh = y[:, h * B_DK:(h + 1) * B_DK]
        ss = jnp.sum(yh * yh, axis=-1, keepdims=True)
        nh = yh * (lax.rsqrt(ss + EPS) * norm_scale)
        o_ref[:, h * B_DK:(h + 1) * B_DK] = jnp.where(j < 2, nh, yh).astype(o_ref.dtype)


def _gdn_prep(proj, conv_w, batch, seq):
    return pl.pallas_call(
        _gdn_prep_kernel,
        grid=(batch, 3),
        in_specs=[pl.BlockSpec((seq, W_GDN), lambda b, j: (b, j)),
                  pl.BlockSpec((B_CONV, W_GDN), lambda b, j: (0, j))],
        out_specs=pl.BlockSpec((seq, W_GDN), lambda b, j: (b, j)),
        out_shape=jax.ShapeDtypeStruct((batch * seq, 3 * W_GDN), BF16),
        compiler_params=_cparams(("parallel", "parallel")),
        name="gdn_prep",
    )(proj, conv_w)


def _lane_block(width, block, h):
    lane = lax.broadcasted_iota(jnp.int32, (1, width), 1)
    return (lane >= h * block) & (lane < (h + 1) * block)


def _block_diag(x, block):
    width = x.shape[1]
    zero = jnp.zeros_like(x)
    return jnp.concatenate([jnp.where(_lane_block(width, block, h), x, zero) for h in range(N_HEADS)], axis=0)


def _unit_tri_inverse_cat(a_list, eye_t):
    p = [eye_t - a for a in a_list]
    ab = [a.astype(BF16) for a in a_list]
    ap = [_dot(x, _block_diag(x, CHUNK)) for x in ab]
    for lvl in range(5):
        apb = [x.astype(BF16) for x in ap]
        bd = [_block_diag(x, CHUNK) for x in apb]
        if lvl < 4:
            r = [_dot(jnp.concatenate([pi.astype(BF16), xi], axis=0), bi) for pi, xi, bi in zip(p, apb, bd)]
            p = [pi + ri[0:CHUNK] for pi, ri in zip(p, r)]
            ap = [ri[CHUNK:2 * CHUNK] for ri in r]
        else:
            p = [pi + _dot(pi.astype(BF16), bi) for pi, bi in zip(p, bd)]
    return p


def _chunk_cumsum(g, rev):
    ridx = lax.broadcasted_iota(jnp.int32, g.shape, 0)
    x = g
    s = 1
    while s < CHUNK:
        if rev:
            sh = jnp.where(ridx < CHUNK - s, pltpu.roll(x, CHUNK - s, 0), 0.0)
        else:
            sh = jnp.where(ridx >= s, pltpu.roll(x, s, 0), 0.0)
        x = x + sh
        s *= 2
    return x


def _gdn_chunk_operands(qkv_ref, ab_ref, alog, dtb, rows, rev):
    c0 = N_HEADS if rev else 0
    ab = ab_ref[rows, :]
    xg = ab + dtb
    softplus = jnp.maximum(xg, 0.0) + jnp.log(1.0 + jnp.exp(-jnp.abs(xg)))
    g = -jnp.exp(alog) * softplus
    beta = _sigmoid(ab)
    gc = _chunk_cumsum(g, rev)
    g_last = gc[0:1, :] if rev else gc[CHUNK - 1:CHUNK, :]
    eg = jnp.exp(gc)
    e_rest = jnp.exp(g_last - gc)

    wcat = N_HEADS * CHUNK
    ii = lax.broadcasted_iota(jnp.int32, (CHUNK, 1), 0)
    jl = lax.broadcasted_iota(jnp.int32, (1, wcat), 1) & (CHUNK - 1)
    eye_b = ii == jl
    incl = (ii <= jl) if rev else (ii >= jl)
    g_col = jnp.zeros((CHUNK, wcat), F32)
    for h in range(N_HEADS):
        g_col = jnp.where(_lane_block(wcat, CHUNK, h), gc[:, c0 + h:c0 + h + 1], g_col)
    g_row = jnp.sum(jnp.where(eye_b, g_col, 0.0), axis=0, keepdims=True)
    dec = jnp.exp(jnp.where(incl, g_col - g_row, NEG))

    kb, vb, kbg, qg, kd = [], [], [], [], []
    for h in range(N_HEADS):
        c = c0 + h
        qh = qkv_ref[rows, h * B_DK:(h + 1) * B_DK].astype(F32)
        kh = qkv_ref[rows, W_GDN + h * B_DK:W_GDN + (h + 1) * B_DK].astype(F32)
        vh = qkv_ref[rows, 2 * W_GDN + h * B_DK:2 * W_GDN + (h + 1) * B_DK].astype(F32)
        bcol = beta[:, 8 + c:9 + c]
        egc = eg[:, c:c + 1]
        kbh = kh * bcol
        kb.append(kbh)
        vb.append(vh * bcol)
        kbg.append(kbh * egc)
        qg.append(qh * egc)
        kd.append((kh * e_rest[:, c:c + 1]).astype(BF16))
    cat = lambda xs: jnp.concatenate(xs, axis=1)
    return dict(dec=dec, eye_b=eye_b, kb=cat(kb).astype(BF16), vb=cat(vb).astype(BF16), kbg=cat(kbg).astype(BF16),
                qg=cat(qg), kd=kd, eg_last=jnp.exp(g_last),
                k=qkv_ref[rows, W_GDN:2 * W_GDN], q=qkv_ref[rows, 0:W_GDN])


def _gdn_solve_chunks(ops):
    kq = [_dot_nt(jnp.concatenate([o["kb"], o["q"]], axis=0), _block_diag(o["k"], B_DK)) for o in ops]
    a_cat = [jnp.where(o["eye_b"], 0.0, x[0:CHUNK] * o["dec"]) for o, x in zip(ops, kq)]
    qk = [(x[CHUNK:2 * CHUNK] * o["dec"]).astype(BF16) for o, x in zip(ops, kq)]
    eye_t = jnp.where(ops[0]["eye_b"], 1.0, 0.0).astype(F32)
    t_cat = [t.astype(BF16) for t in _unit_tri_inverse_cat(a_cat, eye_t)]
    ub = [_dot(t, _block_diag(o["vb"], B_DK)).astype(BF16) for o, t in zip(ops, t_cat)]
    wb = [_dot(t, _block_diag(o["kbg"], B_DK)).astype(BF16) for o, t in zip(ops, t_cat)]
    q_eff = [o["qg"] - _dot(x, _block_diag(w, B_DK)) for o, x, w in zip(ops, qk, wb)]
    o_intra = [_dot(x, _block_diag(u, B_DK)) for x, u in zip(qk, ub)]
    mc = []
    for o, u, w in zip(ops, ub, wb):
        per_head = []
        for h in range(N_HEADS):
            cols = slice(h * B_DK, (h + 1) * B_DK)
            per_head.append(_dot_tn(o["kd"][h], jnp.concatenate([w[:, cols], u[:, cols]], axis=1)))
        mc.append(per_head)
    return mc, q_eff, o_intra


def _gdn_kernel(qkv_ref, ab_ref, alog_ref, dtb_ref, o_ref, s_ref, m_ref, c_ref, qe_ref, oi_ref, egl_ref, *, rev, group):
    seq = qkv_ref.shape[0]
    nc = seq // CHUNK
    c0 = N_HEADS if rev else 0
    alog = alog_ref[...]
    dtb = dtb_ref[...]

    def solve(i, carry):
        ns = [i * group + j for j in range(group)]
        rows = [pl.ds(pl.multiple_of(n * CHUNK, CHUNK), CHUNK) for n in ns]
        ops = [_gdn_chunk_operands(qkv_ref, ab_ref, alog, dtb, r, rev) for r in rows]
        mc, q_eff, o_intra = _gdn_solve_chunks(ops)
        for j, n in enumerate(ns):
            for h in range(N_HEADS):
                m_ref[n * N_HEADS + h] = mc[j][h][:, 0:B_DK].astype(BF16)
                c_ref[n * N_HEADS + h] = mc[j][h][:, B_DK:2 * B_DK]
            qe_ref[rows[j], :] = q_eff[j].astype(BF16)
            oi_ref[rows[j], :] = o_intra[j]
            egl_ref[n] = jnp.broadcast_to(ops[j]["eg_last"], (8, 128))
        return carry

    lax.fori_loop(0, nc // group, solve, 0)

    s_ref[...] = jnp.zeros(s_ref.shape, F32)

    def scan(i, carry):
        n = (nc - 1 - i) if rev else i
        rows = pl.ds(pl.multiple_of(n * CHUNK, CHUNK), CHUNK)
        egl = egl_ref[n]
        new_s, new_o = [], []
        for h in range(N_HEADS):
            cols = slice(h * B_DK, (h + 1) * B_DK)
            state = s_ref[h]
            lhs = jnp.concatenate([m_ref[n * N_HEADS + h], qe_ref[rows, cols]], axis=0)
            r = _dot(lhs, state.astype(BF16))
            new_s.append(state * egl[0:1, c0 + h:c0 + h + 1] - r[0:B_DK] + c_ref[n * N_HEADS + h])
            new_o.append(oi_ref[rows, cols] + r[B_DK:B_DK + CHUNK])
        for h in range(N_HEADS):
            s_ref[h] = new_s[h]
            o_ref[rows, h * B_DK:(h + 1) * B_DK] = new_o[h].astype(o_ref.dtype)
        return carry

    lax.fori_loop(0, nc, scan, 0)


def _gdn(qkv, ab, alog, dtb, batch, seq, rev, group=8):
    nc = seq // CHUNK
    return pl.pallas_call(
        functools.partial(_gdn_kernel, rev=rev, group=group),
        grid=(batch,),
        in_specs=[pl.BlockSpec((seq, 3 * W_GDN), lambda b: (b, 0)),
                  pl.BlockSpec((seq, 128), lambda b: (b, 0)),
                  pl.BlockSpec((1, 128), lambda b: (0, 0)),
                  pl.BlockSpec((1, 128), lambda b: (0, 0))],
        out_specs=pl.BlockSpec((seq, W_GDN), lambda b: (b, 0)),
        out_shape=jax.ShapeDtypeStruct((batch * seq, W_GDN), BF16),
        scratch_shapes=[pltpu.VMEM((N_HEADS, B_DK, B_DK), F32),
                        pltpu.VMEM((nc * N_HEADS, B_DK, B_DK), BF16),
                        pltpu.VMEM((nc * N_HEADS, B_DK, B_DK), F32),
                        pltpu.VMEM((seq, W_GDN), BF16),
                        pltpu.VMEM((seq, W_GDN), F32),
                        pltpu.VMEM((nc, 8, 128), F32)],
        compiler_params=_cparams(("parallel",)),
        name="gdn_bwd" if rev else "gdn_fwd",
    )(qkv, ab, alog, dtb)


def _ffn_kernel(x_ref, oa_ref, of_ref, ob_ref, z_ref, oc_ref, onorm_ref, wa_ref, wb_ref, wc_ref,
                nw_ref, wg_ref, wu_ref, wd_ref, o_ref, xn_ref, acc_ref):
    j = pl.program_id(1)

    @pl.when(j == 0)
    def _():
        x = x_ref[...] + _dot(_join_halves([oa_ref[0], oa_ref[1]]).astype(BF16), wa_ref[...])
        x = x + _dot(oc_ref[...], wc_ref[...])
        for h in range(N_HEADS):
            cols = slice(h * B_DK, (h + 1) * B_DK)
            o = of_ref[:, cols].astype(F32) + ob_ref[:, cols].astype(F32)
            ms = jnp.mean(o * o, axis=-1, keepdims=True)
            y = (o * lax.rsqrt(ms + EPS)) * onorm_ref[...]
            gated = (y * _silu(z_ref[:, cols].astype(F32))).astype(BF16)
            x = x + _dot(gated, wb_ref[h * B_DK:(h + 1) * B_DK, :])
        ms = jnp.mean(x * x, axis=-1, keepdims=True)
        xn_ref[...] = ((x * lax.rsqrt(ms + EPS)) * nw_ref[...]).astype(BF16)
        acc_ref[...] = x

    hr = xn_ref.shape[0] // 2
    rsl = [slice(h * hr, (h + 1) * hr) for h in range(2)]
    xns = [xn_ref[r, :] for r in rsl]
    gates = [_dot(xn, wg_ref[...]) for xn in xns]
    ups = [_dot(xn, wu_ref[...]) for xn in xns]
    hiddens = [(_silu(g) * u).astype(BF16) for g, u in zip(gates, ups)]
    downs = [_dot(hd, wd_ref[...]) for hd in hiddens]
    for r, dn in zip(rsl, downs):
        acc_ref[r, :] += dn

    @pl.when(j == pl.num_programs(1) - 1)
    def _():
        o_ref[...] = acc_ref[...]


def _out_proj_ffn(x, oa, o_f, o_b, proj_b, oc, onorm, wa, wb, wc, nw, w_gate_up, w_down, tm=512, tf=2816):
    t, d = x.shape
    dff = w_down.shape[0]
    nf = dff // tf
    row = lambda w: pl.BlockSpec((tm, w), lambda i, j: (i, 0))
    full = lambda w: pl.BlockSpec((w, d), lambda i, j: (0, 0))
    return pl.pallas_call(
        _ffn_kernel,
        grid=(t // tm, nf),
        in_specs=[row(d), pl.BlockSpec((2, tm, 128), lambda i, j: (0, i, 0)), row(W_GDN), row(W_GDN),
                  pl.BlockSpec((tm, W_GDN), lambda i, j: (i, COL_BZ // W_GDN)),
                  row(W_ATT), pl.BlockSpec((1, B_DK), lambda i, j: (0, 0)),
                  full(W_ATT), full(W_GDN), full(W_ATT),
                  pl.BlockSpec((1, d), lambda i, j: (0, 0)),
                  pl.BlockSpec((d, tf), lambda i, j: (0, j), pipeline_mode=pl.Buffered(1)),
                  pl.BlockSpec((d, tf), lambda i, j: (0, j + nf), pipeline_mode=pl.Buffered(1)),
                  pl.BlockSpec((tf, d), lambda i, j: (j, 0), pipeline_mode=pl.Buffered(1))],
        out_specs=pl.BlockSpec((tm, d), lambda i, j: (i, 0)),
        out_shape=jax.ShapeDtypeStruct((t, d), F32),
        scratch_shapes=[pltpu.VMEM((tm, d), BF16), pltpu.VMEM((tm, d), F32)],
        compiler_params=_cparams(("parallel", "arbitrary")),
        name="ffn",
    )(x, oa, o_f, o_b, proj_b, oc, onorm, wa, wb, wc, nw, w_gate_up, w_gate_up, w_down)


def _rope_tables(seq):
    t = np.arange(seq, dtype=np.float64)

    def build(groups):
        cos = np.ones((seq, HEAD_DIM))
        s_lo = np.zeros((seq, HEAD_DIM))
        s_hi = np.zeros((seq, HEAD_DIM))
        for start, half, theta, pos in groups:
            inv = np.float64(np.float32(theta)) ** (-np.arange(half, dtype=np.float64) / half)
            ang = pos[:, None] * inv[None, :]
            cos[:, start:start + half] = np.cos(ang)
            cos[:, start + half:start + 2 * half] = np.cos(ang)
            s_lo[:, start:start + half] = -np.sin(ang)
            s_hi[:, start + half:start + 2 * half] = np.sin(ang)
        tab = np.stack([cos, s_lo, s_hi])
        return np.tile(tab, (1, 1, N_HEADS)).astype(np.float32)

    tab_a = build([(0, A_ROT_HALF, A_THETA, t)])
    row = np.floor(t / GRID_W)
    col = t - row * GRID_W
    tab_c = build([(0, C_ROT_HALF, C_THETA, row), (2 * C_ROT_HALF, C_ROT_HALF, C_THETA, col)])
    return tab_a, tab_c


def _multiplicity_tables(seq):
    (w0, d0), rest = A_PATTERNS[0], A_PATTERNS[1:]
    assert d0 == 1 and all(d % N_CLS == 0 for _, d in rest)
    r0 = w0 // 2
    tq, win = A_LOCAL_TQ, A_LOCAL_TQ + 2 * r0
    qi = np.arange(tq)[:, None]
    kj = np.arange(win)[None, :]
    local = np.stack([(np.abs(qi + off - kj) <= r0) for off in (0, r0, 2 * r0)]).astype(np.float32)
    n = seq // N_CLS
    d = np.arange(n)[:, None] - np.arange(n)[None, :]
    dilated = np.zeros((n, n), np.float32)
    for window, dil in rest:
        step = dil // N_CLS
        radius = window // (2 * dil)
        dilated += ((d % step == 0) & (np.abs(d) <= radius * step)).astype(np.float32)
    return local, dilated


def _head_mean_matrix():
    e = np.kron(np.eye(N_HEADS), np.full((HEAD_DIM, HEAD_DIM), 1.0 / HEAD_DIM))
    return e.astype(np.float32)


def _pad_lanes(v, width):
    return jnp.pad(v.reshape(1, -1), ((0, 0), (0, width - v.size)))


def kernel(x, norm1, w_in, qn_a, kn_a, conv_b, a_log_b, dt_bias_b, onorm_b, qn_c, kn_c, w_out, norm2, w_gate_up, w_down):
    batch, seq, d = x.shape
    depth = w_in.shape[0]
    t = batch * seq

    tab_a, tab_c = _rope_tables(seq)
    tab_a, tab_c = jnp.asarray(tab_a), jnp.asarray(tab_c)
    mult_local, mult_dil = _multiplicity_tables(seq)
    mult_local, mult_dil = jnp.asarray(mult_local, dtype=BF16), jnp.asarray(mult_dil, dtype=BF16)
    e_mat = jnp.asarray(_head_mean_matrix(), dtype=BF16)

    c0 = 3 * W_ATT
    c1 = c0 + 4 * W_GDN
    c2 = c1 + 4 * N_HEADS
    cq = [w_in[:, :, c2 + h * HEAD_DIM:c2 + (h + 1) * HEAD_DIM] for h in C_HEAD_ORDER]
    w_in_p = jnp.concatenate(
        [w_in[:, :, c0:c1], w_in[:, :, 0:c0]] + cq + [w_in[:, :, c2 + W_ATT:], w_in[:, :, c1:c2]], axis=-1)
    w_in_p = jnp.pad(w_in_p, ((0, 0), (0, 0), (0, IN_PAD - w_in_p.shape[-1]))).astype(BF16)

    w_out_a = w_out[:, 0:W_ATT].astype(BF16)
    w_out_b = w_out[:, W_ATT:W_ATT + W_GDN].astype(BF16)
    wc = w_out[:, W_ATT + W_GDN:]
    w_out_c = jnp.concatenate([wc[:, h * HEAD_DIM:(h + 1) * HEAD_DIM] for h in C_HEAD_ORDER], axis=1).astype(BF16)
    w_gu = w_gate_up.astype(BF16)
    w_dn = w_down.astype(BF16)

    tile4 = lambda v: jnp.tile(v.reshape(1, -1), (1, N_HEADS))

    xt = x.reshape(t, d)
    for i in range(depth):
        proj, proj_ab, (qa_cls, ka_cls, va_cls), (qa, ka, va, qc, kc, vc) = _in_proj(
            xt, norm1[i].reshape(1, d), w_in_p[i], tile4(qn_a[i]), tile4(kn_a[i]), tile4(qn_c[i]), tile4(kn_c[i]),
            tab_a, tab_c, e_mat, batch, seq)
        o1, lse1 = _attn_a_local(qa, ka, va, mult_local, batch, seq)
        o_a = _attn_a_dilated(qa_cls, ka_cls, va_cls, mult_dil, o1, lse1, batch, seq)
        o_c = _attn_c(qc, kc, vc, batch, seq)
        qkv_b = _gdn_prep(proj, conv_b[i], batch, seq)
        alog, dtb = _pad_lanes(a_log_b[i], 128), _pad_lanes(dt_bias_b[i], 128)
        o_f = _gdn(qkv_b, proj_ab, alog, dtb, batch, seq, rev=False)
        o_r = _gdn(qkv_b, proj_ab, alog, dtb, batch, seq, rev=True)
        xt = _out_proj_ffn(xt, o_a, o_f, o_r, proj, o_c, onorm_b[i].reshape(1, B_DK),
                           w_out_a[i], w_out_b[i], w_out_c[i], norm2[i].reshape(1, d), w_gu[i], w_dn[i])
    return xt.reshape(batch, seq, d)
```

```python
import functools

import numpy as np
import jax
import jax.numpy as jnp
from jax import lax
from jax.experimental import pallas as pl
from jax.experimental.pallas import tpu as pltpu

F32 = jnp.float32
BF16 = jnp.bfloat16

EPS = 1e-6
GRID_W = 64
HEAD_DIM = 64
N_HEADS = 4
A_PATTERNS = ((128, 1), (512, 4), (2048, 16))
A_ROT_HALF = 8
A_THETA = 500000.0
C_ROT_HALF = 16
C_THETA = 10000.0
B_DK = 128
B_CONV = 5
CHUNK = 64
NEG = -1e30

W_ATT = N_HEADS * HEAD_DIM
W_GDN = N_HEADS * B_DK
COL_BQ, COL_BK, COL_BV, COL_BZ = 0, 512, 1024, 1536
COL_AQ, COL_AK, COL_AV = 2048, 2304, 2560
COL_CQ, COL_CK, COL_CV = 2816, 3072, 3200
COL_AB = 3328
IN_PAD = 3456
N_CLS = 4
A_LOCAL_TQ = 128
C_HEAD_ORDER = (0, 2, 1, 3)

VMEM_LIMIT = 56 * 1024 * 1024


def _cparams(sem):
    return pltpu.CompilerParams(dimension_semantics=sem, vmem_limit_bytes=VMEM_LIMIT)


def _split2(x):
    hi = x.astype(BF16)
    lo = (x - hi.astype(F32)).astype(BF16)
    return hi, lo


def _dot(a, b):
    return jnp.dot(a, b, preferred_element_type=F32)


def _dot_nt(a, b):
    return lax.dot_general(a, b, (((1,), (1,)), ((), ())), preferred_element_type=F32)


def _dot_tn(a, b):
    return lax.dot_general(a, b, (((0,), (0,)), ((), ())), preferred_element_type=F32)


def _sigmoid(x):
    return 1.0 / (1.0 + jnp.exp(-x))


def _silu(x):
    hx = 0.5 * x
    return hx + hx * jnp.tanh(hx)


def _head_norm_rope(x, w, e_hi, cos, sin_lo, sin_hi, rot_half, scale):
    xx = x * x
    hi, lo = _split2(xx)
    ms = _dot(hi, e_hi) + _dot(lo, e_hi)
    y = (x * lax.rsqrt(ms + EPS)) * w
    width = x.shape[1]
    up = pltpu.roll(y, width - rot_half, 1)
    dn = pltpu.roll(y, rot_half, 1)
    out = y * cos + up * sin_lo + dn * sin_hi
    return out * scale


def _join_halves(halves):
    return jnp.concatenate(halves, axis=1)


def _inproj_kernel(x_ref, nw_ref, w_ref, qna_ref, kna_ref, qnc_ref, knc_ref, ta_ref, tc_ref, e_ref,
                   ob_ref, oab_ref, oqa_cls_ref, oka_cls_ref, ova_cls_ref, oqa_ref, oka_ref, ova_ref,
                   oqc_ref, okc_ref, ovc_ref, scr_ref):
    tm = x_ref.shape[0]
    nh = 2
    hr = tm // nh
    rsl = [slice(h * hr, (h + 1) * hr) for h in range(nh)]
    ys = []
    for r in rsl:
        x = x_ref[r, :]
        ms = jnp.mean(x * x, axis=-1, keepdims=True)
        ys.append(((x * lax.rsqrt(ms + EPS)) * nw_ref[...]).astype(BF16))
    ress = [_dot(y, w_ref[...]) for y in ys]

    e = e_ref[...]
    scale = HEAD_DIM ** -0.5
    crows = hr // N_CLS
    for h, (r, res) in enumerate(zip(rsl, ress)):
        ob_ref[r, :] = res[:, COL_BQ:COL_AQ].astype(BF16)
        oab_ref[r, :] = res[:, COL_AB:IN_PAD]
        cos_a, sl_a, sh_a = ta_ref[0, r, :], ta_ref[1, r, :], ta_ref[2, r, :]
        cos_c, sl_c, sh_c = tc_ref[0, r, :], tc_ref[1, r, :], tc_ref[2, r, :]

        def emit(v, tok_ref, cls_ref):
            tok_ref[r, :] = v.astype(BF16)
            for j in range(2):
                scr_ref[h, j] = v[:, j * 128:(j + 1) * 128]
            for c in range(N_CLS):
                grouped = _join_halves([scr_ref[h, j, pl.ds(c, crows, stride=N_CLS), :] for j in range(2)])
                cls_ref[c, h * crows:(h + 1) * crows, :] = grouped.astype(BF16)

        col = lambda c0, w: res[:, c0:c0 + w]
        emit(_head_norm_rope(col(COL_AQ, 256), qna_ref[...], e, cos_a, sl_a, sh_a, A_ROT_HALF, scale), oqa_ref, oqa_cls_ref)
        emit(_head_norm_rope(col(COL_AK, 256), kna_ref[...], e, cos_a, sl_a, sh_a, A_ROT_HALF, 1.0), oka_ref, oka_cls_ref)
        emit(col(COL_AV, 256), ova_ref, ova_cls_ref)
        oqc_ref[r, :] = _head_norm_rope(col(COL_CQ, 256), qnc_ref[...], e, cos_c, sl_c, sh_c, C_ROT_HALF, scale).astype(BF16)
        okc_ref[r, :] = _head_norm_rope(col(COL_CK, 128), knc_ref[...][:, :128], e[:128, :128],
                                        cos_c[:, :128], sl_c[:, :128], sh_c[:, :128], C_ROT_HALF, 1.0).astype(BF16)
        ovc_ref[r, :] = col(COL_CV, 128).astype(BF16)


def _in_proj(x, nw, w, qna, kna, qnc, knc, tab_a, tab_c, e_mat, batch, seq, tm=512):
    t, d = x.shape
    n = w.shape[1]
    ns = seq // tm
    vec = pl.BlockSpec((1, W_ATT), lambda i: (0, 0))
    tab = pl.BlockSpec((3, tm, W_ATT), lambda i: (0, i % ns, 0))
    ob = lambda wd: pl.BlockSpec((tm, wd), lambda i: (i, 0))
    cls = pl.BlockSpec((None, N_CLS, tm // N_CLS, W_ATT), lambda i: (i // ns, 0, i % ns, 0))
    widths = (256, 256, 256, 256, 128, 128)
    outs = pl.pallas_call(
        _inproj_kernel,
        grid=(t // tm,),
        in_specs=[pl.BlockSpec((tm, d), lambda i: (i, 0)),
                  pl.BlockSpec((1, d), lambda i: (0, 0)),
                  pl.BlockSpec((d, n), lambda i: (0, 0)),
                  vec, vec, vec, vec, tab, tab,
                  pl.BlockSpec((W_ATT, W_ATT), lambda i: (0, 0))],
        out_specs=[ob(COL_AQ), ob(n - COL_AB), cls, cls, cls] + [ob(wd) for wd in widths],
        out_shape=[jax.ShapeDtypeStruct((t, COL_AQ), BF16), jax.ShapeDtypeStruct((t, n - COL_AB), F32)]
                  + [jax.ShapeDtypeStruct((batch, N_CLS, seq // N_CLS, W_ATT), BF16)] * 3
                  + [jax.ShapeDtypeStruct((t, wd), BF16) for wd in widths],
        scratch_shapes=[pltpu.VMEM((2, 2, tm // 2, 128), F32)],
        compiler_params=_cparams(("parallel",)),
        name="in_proj",
    )(x, nw, w, qna, kna, qnc, knc, tab_a, tab_c, e_mat)
    return outs[0], outs[1], outs[2:5], outs[5:]


def _head_masks(width):
    lane = lax.broadcasted_iota(jnp.int32, (1, width), 1)
    return [(lane >= h * HEAD_DIM) & (lane < (h + 1) * HEAD_DIM) for h in range(width // HEAD_DIM)]


def _attn_a_local_kernel(q_ref, k_ref, v_ref, m_ref, o1_ref, lse_ref):
    i = pl.program_id(1)
    seq = k_ref.shape[0]
    tq, win = m_ref.shape[1], m_ref.shape[2]
    nblk = q_ref.shape[0] // tq
    last = seq // tq - 1
    hms = _head_masks(W_ATT)
    vs, scores = [], []
    for j in range(nblk):
        blk = i * nblk + j
        start = jnp.clip(blk * tq - (win - tq) // 2, 0, seq - win)
        rows = pl.ds(pl.multiple_of(start, 64), win)
        k = k_ref[rows, :]
        vs.append(v_ref[rows, :])
        mult = m_ref[1]
        if j == 0:
            mult = jnp.where(blk == 0, m_ref[0], mult)
        if j == nblk - 1:
            mult = jnp.where(blk == last, m_ref[2], mult)
        q = q_ref[j * tq:(j + 1) * tq, :]
        scores.append([jnp.where(mult > 0, _dot_nt(jnp.where(hm, q, jnp.zeros_like(q)), k), NEG) for hm in hms])
    maxes = [[jnp.max(s, axis=-1, keepdims=True) for s in row] for row in scores]
    probs = [[jnp.exp(s - m) for s, m in zip(srow, mrow)] for srow, mrow in zip(scores, maxes)]
    sums = [[jnp.sum(p, axis=-1, keepdims=True) for p in row] for row in probs]
    pvs = [[_dot(p.astype(BF16), v) for p in row] for row, v in zip(probs, vs)]
    for j in range(nblk):
        out = jnp.zeros((tq, W_ATT), F32)
        lse = jnp.zeros((tq, W_ATT), F32)
        for h in range(N_HEADS):
            out = jnp.where(hms[h], pvs[j][h] * (1.0 / sums[j][h]), out)
            lse = jnp.where(hms[h], maxes[j][h] + jnp.log(sums[j][h]), lse)
        for half in range(2):
            o1_ref[half, j * tq:(j + 1) * tq, :] = out[:, half * 128:(half + 1) * 128]
            lse_ref[half, j * tq:(j + 1) * tq, :] = lse[:, half * 128:(half + 1) * 128]


def _attn_a_local(q, k, v, mult, batch, seq, nblk=4):
    tq = mult.shape[1]
    rows = nblk * tq
    nq = seq // rows
    return pl.pallas_call(
        _attn_a_local_kernel,
        grid=(batch, nq),
        in_specs=[pl.BlockSpec((rows, W_ATT), lambda b, i: (b * nq + i, 0)),
                  pl.BlockSpec((seq, W_ATT), lambda b, i: (b, 0)),
                  pl.BlockSpec((seq, W_ATT), lambda b, i: (b, 0)),
                  pl.BlockSpec(mult.shape, lambda b, i: (0, 0, 0))],
        out_specs=[pl.BlockSpec((2, rows, 128), lambda b, i: (0, b * nq + i, 0)),
                   pl.BlockSpec((2, rows, 128), lambda b, i: (0, b * nq + i, 0))],
        out_shape=[jax.ShapeDtypeStruct((2, batch * seq, 128), F32), jax.ShapeDtypeStruct((2, batch * seq, 128), F32)],
        compiler_params=_cparams(("parallel", "parallel")),
        name="attn_a_local",
    )(q, k, v, mult)


def _attn_a_dilated_kernel(q_ref, k_ref, v_ref, m_ref, o1_ref, lse1_ref, o_ref):
    tq = q_ref.shape[1]
    mult = m_ref[...]
    multf = mult.astype(F32)
    hms = _head_masks(W_ATT)
    qs = [q_ref[c] for c in range(N_CLS)]
    scores = [[jnp.where(mult > 0, _dot_nt(jnp.where(hm, q, jnp.zeros_like(q)), k_ref[c]), NEG) for hm in hms]
              for c, q in enumerate(qs)]
    m2 = [[jnp.max(s, axis=-1, keepdims=True) for s in row] for row in scores]
    probs = [[jnp.exp(s - m) * multf for s, m in zip(srow, mrow)] for srow, mrow in zip(scores, m2)]
    l2 = [[jnp.sum(p, axis=-1, keepdims=True) for p in row] for row in probs]
    pv2 = [[_dot(p.astype(BF16), v_ref[c]) for p in row] for c, row in enumerate(probs)]
    for c in range(N_CLS):
        m2b = jnp.zeros((tq, W_ATT), F32)
        l2b = jnp.zeros((tq, W_ATT), F32)
        pv2b = jnp.zeros((tq, W_ATT), F32)
        for h in range(N_HEADS):
            m2b = jnp.where(hms[h], m2[c][h], m2b)
            l2b = jnp.where(hms[h], l2[c][h], l2b)
            pv2b = jnp.where(hms[h], pv2[c][h], pv2b)
        tok = pl.ds(c, tq, stride=N_CLS)
        lse1 = _join_halves([lse1_ref[half, tok, :] for half in range(2)])
        o1 = _join_halves([o1_ref[half, tok, :] for half in range(2)])
        m = jnp.maximum(lse1, m2b)
        a1 = jnp.exp(lse1 - m)
        a2 = jnp.exp(m2b - m)
        out = (o1 * a1 + pv2b * a2) / (a1 + l2b * a2)
        for half in range(2):
            o_ref[half, tok, :] = out[:, half * 128:(half + 1) * 128]


def _attn_a_dilated(q, k, v, mult, o1, lse1, batch, seq, tq=128):
    rows = seq // N_CLS
    nu = rows // tq
    tok = pl.BlockSpec((2, N_CLS * tq, 128), lambda u, b: (0, b * nu + u, 0))
    return pl.pallas_call(
        _attn_a_dilated_kernel,
        grid=(nu, batch),
        in_specs=[pl.BlockSpec((None, N_CLS, tq, W_ATT), lambda u, b: (b, 0, u, 0)),
                  pl.BlockSpec((None, N_CLS, rows, W_ATT), lambda u, b: (b, 0, 0, 0)),
                  pl.BlockSpec((None, N_CLS, rows, W_ATT), lambda u, b: (b, 0, 0, 0)),
                  pl.BlockSpec((tq, rows), lambda u, b: (u, 0)),
                  tok, tok],
        out_specs=tok,
        out_shape=jax.ShapeDtypeStruct((2, batch * seq, 128), F32),
        compiler_params=_cparams(("parallel", "parallel")),
        name="attn_a_dilated",
    )(q, k, v, mult, o1, lse1)


def _attn_c_kernel(q_ref, k_ref, v_ref, o_ref):
    k = k_ref[...]
    v = v_ref[...]
    hms = _head_masks(128)
    qs = [q_ref[:, pair * 128:(pair + 1) * 128] for pair in range(2)]
    scores = [[_dot_nt(jnp.where(hm, q, jnp.zeros_like(q)), k) for hm in hms] for q in qs]
    maxes = [[jnp.max(s, axis=-1, keepdims=True) for s in row] for row in scores]
    probs = [[jnp.exp(s - m) for s, m in zip(srow, mrow)] for srow, mrow in zip(scores, maxes)]
    sums = [[jnp.sum(p, axis=-1, keepdims=True) for p in row] for row in probs]
    pvs = [[_dot(p.astype(BF16), v) for p in row] for row in probs]
    for pair in range(2):
        acc = jnp.zeros(qs[pair].shape, F32)
        for half in range(2):
            acc = jnp.where(hms[half], pvs[pair][half] * (1.0 / sums[pair][half]), acc)
        o_ref[:, pair * 128:(pair + 1) * 128] = acc.astype(o_ref.dtype)


def _attn_c(q, k, v, batch, seq, tq=512):
    nq = seq // tq
    return pl.pallas_call(
        _attn_c_kernel,
        grid=(batch, nq),
        in_specs=[pl.BlockSpec((tq, W_ATT), lambda b, i: (b * nq + i, 0)),
                  pl.BlockSpec((seq, 128), lambda b, i: (b, 0)),
                  pl.BlockSpec((seq, 128), lambda b, i: (b, 0))],
        out_specs=pl.BlockSpec((tq, W_ATT), lambda b, i: (b * nq + i, 0)),
        out_shape=jax.ShapeDtypeStruct((batch * seq, W_ATT), BF16),
        compiler_params=_cparams(("parallel", "parallel")),
        name="attn_c",
    )(q, k, v)


def _gdn_prep_kernel(x_ref, w_ref, o_ref):
    j = pl.program_id(1)
    x = x_ref[...].astype(F32)
    w = w_ref[...]
    seq = x.shape[0]
    pos = lax.broadcasted_iota(jnp.int32, (seq, 1), 0)
    acc = x * w[B_CONV // 2:B_CONV // 2 + 1, :]
    for tap in range(B_CONV):
        off = tap - B_CONV // 2
        if off == 0:
            continue
        sh = pltpu.roll(x, (-off) % seq, 0)
        ok = (pos + off >= 0) & (pos + off < seq)
        acc = acc + jnp.where(ok, sh, 0.0) * w[tap:tap + 1, :]
    y = _silu(acc)
    norm_scale = jnp.where(j == 0, B_DK ** -0.5, 1.0).astype(F32)
    for h in range(N_HEADS):
        yh = y[:, h * B_DK:(h + 1) * B_DK]
        ss = jnp.sum(yh * yh, axis=-1, keepdims=True)
        nh = yh * (lax.rsqrt(ss + EPS) * norm_scale)
        o_ref[:, h * B_DK:(h + 1) * B_DK] = jnp.where(j < 2, nh, yh).astype(o_ref.dtype)


def _gdn_prep(proj, conv_w, batch, seq):
    return pl.pallas_call(
        _gdn_prep_kernel,
        grid=(batch, 3),
        in_specs=[pl.BlockSpec((seq, W_GDN), lambda b, j: (b, j)),
                  pl.BlockSpec((B_CONV, W_GDN), lambda b, j: (0, j))],
        out_specs=pl.BlockSpec((seq, W_GDN), lambda b, j: (b, j)),
        out_shape=jax.ShapeDtypeStruct((batch * seq, 3 * W_GDN), BF16),
        compiler_params=_cparams(("parallel", "parallel")),
        name="gdn_prep",
    )(proj, conv_w)


def _lane_block(width, block, h):
    lane = lax.broadcasted_iota(jnp.int32, (1, width), 1)
    return (lane >= h * block) & (lane < (h + 1) * block)


def _block_diag(x, block):
    width = x.shape[1]
    zero = jnp.zeros_like(x)
    return jnp.concatenate([jnp.where(_lane_block(width, block, h), x, zero) for h in range(N_HEADS)], axis=0)


def _unit_tri_inverse_cat(a_list, eye_t):
    p = [eye_t - a for a in a_list]
    ab = [a.astype(BF16) for a in a_list]
    ap = [_dot(x, _block_diag(x, CHUNK)) for x in ab]
    for lvl in range(5):
        apb = [x.astype(BF16) for x in ap]
        bd = [_block_diag(x, CHUNK) for x in apb]
        if lvl < 4:
            r = [_dot(jnp.concatenate([pi.astype(BF16), xi], axis=0), bi) for pi, xi, bi in zip(p, apb, bd)]
            p = [pi + ri[0:CHUNK] for pi, ri in zip(p, r)]
            ap = [ri[CHUNK:2 * CHUNK] for ri in r]
        else:
            p = [pi + _dot(pi.astype(BF16), bi) for pi, bi in zip(p, bd)]
    return p


def _chunk_cumsum(g, rev):
    ridx = lax.broadcasted_iota(jnp.int32, g.shape, 0)
    x = g
    s = 1
    while s < CHUNK:
        if rev:
            sh = jnp.where(ridx < CHUNK - s, pltpu.roll(x, CHUNK - s, 0), 0.0)
        else:
            sh = jnp.where(ridx >= s, pltpu.roll(x, s, 0), 0.0)
        x = x + sh
        s *= 2
    return x


def _gdn_chunk_operands(qkv_ref, ab_ref, alog, dtb, rows, rev):
    c0 = N_HEADS if rev else 0
    ab = ab_ref[rows, :]
    xg = ab + dtb
    softplus = jnp.maximum(xg, 0.0) + jnp.log(1.0 + jnp.exp(-jnp.abs(xg)))
    g = -jnp.exp(alog) * softplus
    beta = _sigmoid(ab)
    gc = _chunk_cumsum(g, rev)
    g_last = gc[0:1, :] if rev else gc[CHUNK - 1:CHUNK, :]
    eg = jnp.exp(gc)
    e_rest = jnp.exp(g_last - gc)

    wcat = N_HEADS * CHUNK
    ii = lax.broadcasted_iota(jnp.int32, (CHUNK, 1), 0)
    jl = lax.broadcasted_iota(jnp.int32, (1, wcat), 1) & (CHUNK - 1)
    eye_b = ii == jl
    incl = (ii <= jl) if rev else (ii >= jl)
    g_col = jnp.zeros((CHUNK, wcat), F32)
    for h in range(N_HEADS):
        g_col = jnp.where(_lane_block(wcat, CHUNK, h), gc[:, c0 + h:c0 + h + 1], g_col)
    g_row = jnp.sum(jnp.where(eye_b, g_col, 0.0), axis=0, keepdims=True)
    dec = jnp.exp(jnp.where(incl, g_col - g_row, NEG))

    kb, vb, kbg, qg, kd = [], [], [], [], []
    for h in range(N_HEADS):
        c = c0 + h
        qh = qkv_ref[rows, h * B_DK:(h + 1) * B_DK].astype(F32)
        kh = qkv_ref[rows, W_GDN + h * B_DK:W_GDN + (h + 1) * B_DK].astype(F32)
        vh = qkv_ref[rows, 2 * W_GDN + h * B_DK:2 * W_GDN + (h + 1) * B_DK].astype(F32)
        bcol = beta[:, 8 + c:9 + c]
        egc = eg[:, c:c + 1]
        kbh = kh * bcol
        kb.append(kbh)
        vb.append(vh * bcol)
        kbg.append(kbh * egc)
        qg.append(qh * egc)
        kd.append((kh * e_rest[:, c:c + 1]).astype(BF16))
    cat = lambda xs: jnp.concatenate(xs, axis=1)
    return dict(dec=dec, eye_b=eye_b, kb=cat(kb).astype(BF16), vb=cat(vb).astype(BF16), kbg=cat(kbg).astype(BF16),
                qg=cat(qg), kd=kd, eg_last=jnp.exp(g_last),
                k=qkv_ref[rows, W_GDN:2 * W_GDN], q=qkv_ref[rows, 0:W_GDN])


def _gdn_solve_chunks(ops):
    kq = [_dot_nt(jnp.concatenate([o["kb"], o["q"]], axis=0), _block_diag(o["k"], B_DK)) for o in ops]
    a_cat = [jnp.where(o["eye_b"], 0.0, x[0:CHUNK] * o["dec"]) for o, x in zip(ops, kq)]
    qk = [(x[CHUNK:2 * CHUNK] * o["dec"]).astype(BF16) for o, x in zip(ops, kq)]
    eye_t = jnp.where(ops[0]["eye_b"], 1.0, 0.0).astype(F32)
    t_cat = [t.astype(BF16) for t in _unit_tri_inverse_cat(a_cat, eye_t)]
    ub = [_dot(t, _block_diag(o["vb"], B_DK)).astype(BF16) for o, t in zip(ops, t_cat)]
    wb = [_dot(t, _block_diag(o["kbg"], B_DK)).astype(BF16) for o, t in zip(ops, t_cat)]
    q_eff = [o["qg"] - _dot(x, _block_diag(w, B_DK)) for o, x, w in zip(ops, qk, wb)]
    o_intra = [_dot(x, _block_diag(u, B_DK)) for x, u in zip(qk, ub)]
    mc = []
    for o, u, w in zip(ops, ub, wb):
        per_head = []
        for h in range(N_HEADS):
            cols = slice(h * B_DK, (h + 1) * B_DK)
            per_head.append(_dot_tn(o["kd"][h], jnp.concatenate([w[:, cols], u[:, cols]], axis=1)))
        mc.append(per_head)
    return mc, q_eff, o_intra


def _gdn_kernel(qkv_ref, ab_ref, alog_ref, dtb_ref, of_ref, ob_ref, s_ref, m_ref, c_ref, qe_ref, egl_ref, *, group):
    seq = qkv_ref.shape[0]
    nc = seq // CHUNK
    alog = alog_ref[...]
    dtb = dtb_ref[...]
    o_refs = (of_ref, ob_ref)

    def solve(i, carry):
        ns = [i * group + j for j in range(group)]
        rows = [pl.ds(pl.multiple_of(n * CHUNK, CHUNK), CHUNK) for n in ns]
        for d, rev in enumerate((False, True)):
            ops = [_gdn_chunk_operands(qkv_ref, ab_ref, alog, dtb, r, rev) for r in rows]
            mc, q_eff, o_intra = _gdn_solve_chunks(ops)
            for j, n in enumerate(ns):
                slot = (d * nc + n) * N_HEADS
                for h in range(N_HEADS):
                    m_ref[slot + h] = mc[j][h][:, 0:B_DK].astype(BF16)
                    c_ref[slot + h] = mc[j][h][:, B_DK:2 * B_DK].astype(BF16)
                qe_ref[d, rows[j], :] = q_eff[j].astype(BF16)
                o_refs[d][rows[j], :] = o_intra[j].astype(BF16)
                egl_ref[d * nc + n] = jnp.broadcast_to(ops[j]["eg_last"], (8, 128))
        return carry

    lax.fori_loop(0, nc // group, solve, 0)

    s_ref[...] = jnp.zeros(s_ref.shape, F32)

    def scan(i, carry):
        work = []
        for d, n in enumerate((i, nc - 1 - i)):
            rows = pl.ds(pl.multiple_of(n * CHUNK, CHUNK), CHUNK)
            egl = egl_ref[d * nc + n]
            slot = (d * nc + n) * N_HEADS
            for h in range(N_HEADS):
                cols = slice(h * B_DK, (h + 1) * B_DK)
                state = s_ref[d * N_HEADS + h]
                lhs = jnp.concatenate([m_ref[slot + h], qe_ref[d, rows, cols]], axis=0)
                work.append((d, h, rows, cols, state, egl[0:1, d * N_HEADS + h:d * N_HEADS + h + 1], slot,
                             _dot(lhs, state.astype(BF16))))
        new = []
        for d, h, rows, cols, state, decay, slot, r in work:
            new.append((state * decay - r[0:B_DK] + c_ref[slot + h].astype(F32),
                        o_refs[d][rows, cols].astype(F32) + r[B_DK:B_DK + CHUNK]))
        for (d, h, rows, cols, *_), (st, o) in zip(work, new):
            s_ref[d * N_HEADS + h] = st
            o_refs[d][rows, cols] = o.astype(BF16)
        return carry

    lax.fori_loop(0, nc, scan, 0)


def _gdn(qkv, ab, alog, dtb, batch, seq, group=8):
    nc = seq // CHUNK
    out = pl.BlockSpec((seq, W_GDN), lambda b: (b, 0))
    return pl.pallas_call(
        functools.partial(_gdn_kernel, group=group),
        grid=(batch,),
        in_specs=[pl.BlockSpec((seq, 3 * W_GDN), lambda b: (b, 0)),
                  pl.BlockSpec((seq, 128), lambda b: (b, 0)),
                  pl.BlockSpec((1, 128), lambda b: (0, 0)),
                  pl.BlockSpec((1, 128), lambda b: (0, 0))],
        out_specs=[out, out],
        out_shape=[jax.ShapeDtypeStruct((batch * seq, W_GDN), BF16)] * 2,
        scratch_shapes=[pltpu.VMEM((2 * N_HEADS, B_DK, B_DK), F32),
                        pltpu.VMEM((2 * nc * N_HEADS, B_DK, B_DK), BF16),
                        pltpu.VMEM((2 * nc * N_HEADS, B_DK, B_DK), BF16),
                        pltpu.VMEM((2, seq, W_GDN), BF16),
                        pltpu.VMEM((2 * nc, 8, 128), F32)],
        compiler_params=_cparams(("parallel",)),
        name="gdn",
    )(qkv, ab, alog, dtb)


def _ffn_kernel(x_ref, oa_ref, of_ref, ob_ref, z_ref, oc_ref, onorm_ref, wa_ref, wb_ref, wc_ref,
                nw_ref, wg_ref, wu_ref, wd_ref, o_ref, xn_ref, acc_ref):
    j = pl.program_id(1)

    @pl.when(j == 0)
    def _():
        x = x_ref[...] + _dot(_join_halves([oa_ref[0], oa_ref[1]]).astype(BF16), wa_ref[...])
        x = x + _dot(oc_ref[...], wc_ref[...])
        for h in range(N_HEADS):
            cols = slice(h * B_DK, (h + 1) * B_DK)
            o = of_ref[:, cols].astype(F32) + ob_ref[:, cols].astype(F32)
            ms = jnp.mean(o * o, axis=-1, keepdims=True)
            y = (o * lax.rsqrt(ms + EPS)) * onorm_ref[...]
            gated = (y * _silu(z_ref[:, cols].astype(F32))).astype(BF16)
            x = x + _dot(gated, wb_ref[h * B_DK:(h + 1) * B_DK, :])
        ms = jnp.mean(x * x, axis=-1, keepdims=True)
        xn_ref[...] = ((x * lax.rsqrt(ms + EPS)) * nw_ref[...]).astype(BF16)
        acc_ref[...] = x

    hr = xn_ref.shape[0] // 2
    rsl = [slice(h * hr, (h + 1) * hr) for h in range(2)]
    xns = [xn_ref[r, :] for r in rsl]
    gates = [_dot(xn, wg_ref[...]) for xn in xns]
    ups = [_dot(xn, wu_ref[...]) for xn in xns]
    hiddens = [(_silu(g) * u).astype(BF16) for g, u in zip(gates, ups)]
    downs = [_dot(hd, wd_ref[...]) for hd in hiddens]
    for r, dn in zip(rsl, downs):
        acc_ref[r, :] += dn

    @pl.when(j == pl.num_programs(1) - 1)
    def _():
        o_ref[...] = acc_ref[...]


def _out_proj_ffn(x, oa, o_f, o_b, proj_b, oc, onorm, wa, wb, wc, nw, w_gate_up, w_down, tm=512, tf=2816):
    t, d = x.shape
    dff = w_down.shape[0]
    nf = dff // tf
    row = lambda w: pl.BlockSpec((tm, w), lambda i, j: (i, 0))
    full = lambda w: pl.BlockSpec((w, d), lambda i, j: (0, 0))
    return pl.pallas_call(
        _ffn_kernel,
        grid=(t // tm, nf),
        in_specs=[row(d), pl.BlockSpec((2, tm, 128), lambda i, j: (0, i, 0)), row(W_GDN), row(W_GDN),
                  pl.BlockSpec((tm, W_GDN), lambda i, j: (i, COL_BZ // W_GDN)),
                  row(W_ATT), pl.BlockSpec((1, B_DK), lambda i, j: (0, 0)),
                  full(W_ATT), full(W_GDN), full(W_ATT),
                  pl.BlockSpec((1, d), lambda i, j: (0, 0)),
                  pl.BlockSpec((d, tf), lambda i, j: (0, j), pipeline_mode=pl.Buffered(1)),
                  pl.BlockSpec((d, tf), lambda i, j: (0, j + nf), pipeline_mode=pl.Buffered(1)),
                  pl.BlockSpec((tf, d), lambda i, j: (j, 0), pipeline_mode=pl.Buffered(1))],
        out_specs=pl.BlockSpec((tm, d), lambda i, j: (i, 0)),
        out_shape=jax.ShapeDtypeStruct((t, d), F32),
        scratch_shapes=[pltpu.VMEM((tm, d), BF16), pltpu.VMEM((tm, d), F32)],
        compiler_params=_cparams(("parallel", "arbitrary")),
        name="ffn",
    )(x, oa, o_f, o_b, proj_b, oc, onorm, wa, wb, wc, nw, w_gate_up, w_gate_up, w_down)


def _rope_tables(seq):
    t = np.arange(seq, dtype=np.float64)

    def build(groups):
        cos = np.ones((seq, HEAD_DIM))
        s_lo = np.zeros((seq, HEAD_DIM))
        s_hi = np.zeros((seq, HEAD_DIM))
        for start, half, theta, pos in groups:
            inv = np.float64(np.float32(theta)) ** (-np.arange(half, dtype=np.float64) / half)
            ang = pos[:, None] * inv[None, :]
            cos[:, start:start + half] = np.cos(ang)
            cos[:, start + half:start + 2 * half] = np.cos(ang)
            s_lo[:, start:start + half] = -np.sin(ang)
            s_hi[:, start + half:start + 2 * half] = np.sin(ang)
        tab = np.stack([cos, s_lo, s_hi])
        return np.tile(tab, (1, 1, N_HEADS)).astype(np.float32)

    tab_a = build([(0, A_ROT_HALF, A_THETA, t)])
    row = np.floor(t / GRID_W)
    col = t - row * GRID_W
    tab_c = build([(0, C_ROT_HALF, C_THETA, row), (2 * C_ROT_HALF, C_ROT_HALF, C_THETA, col)])
    return tab_a, tab_c


def _multiplicity_tables(seq):
    (w0, d0), rest = A_PATTERNS[0], A_PATTERNS[1:]
    assert d0 == 1 and all(d % N_CLS == 0 for _, d in rest)
    r0 = w0 // 2
    tq, win = A_LOCAL_TQ, A_LOCAL_TQ + 2 * r0
    qi = np.arange(tq)[:, None]
    kj = np.arange(win)[None, :]
    local = np.stack([(np.abs(qi + off - kj) <= r0) for off in (0, r0, 2 * r0)]).astype(np.float32)
    n = seq // N_CLS
    d = np.arange(n)[:, None] - np.arange(n)[None, :]
    dilated = np.zeros((n, n), np.float32)
    for window, dil in rest:
        step = dil // N_CLS
        radius = window // (2 * dil)
        dilated += ((d % step == 0) & (np.abs(d) <= radius * step)).astype(np.float32)
    return local, dilated


def _head_mean_matrix():
    e = np.kron(np.eye(N_HEADS), np.full((HEAD_DIM, HEAD_DIM), 1.0 / HEAD_DIM))
    return e.astype(np.float32)


def _pad_lanes(v, width):
    return jnp.pad(v.reshape(1, -1), ((0, 0), (0, width - v.size)))


def kernel(x, norm1, w_in, qn_a, kn_a, conv_b, a_log_b, dt_bias_b, onorm_b, qn_c, kn_c, w_out, norm2, w_gate_up, w_down):
    batch, seq, d = x.shape
    depth = w_in.shape[0]
    t = batch * seq

    tab_a, tab_c = _rope_tables(seq)
    tab_a, tab_c = jnp.asarray(tab_a), jnp.asarray(tab_c)
    mult_local, mult_dil = _multiplicity_tables(seq)
    mult_local, mult_dil = jnp.asarray(mult_local, dtype=BF16), jnp.asarray(mult_dil, dtype=BF16)
    e_mat = jnp.asarray(_head_mean_matrix(), dtype=BF16)

    c0 = 3 * W_ATT
    c1 = c0 + 4 * W_GDN
    c2 = c1 + 4 * N_HEADS
    cq = [w_in[:, :, c2 + h * HEAD_DIM:c2 + (h + 1) * HEAD_DIM] for h in C_HEAD_ORDER]
    w_in_p = jnp.concatenate(
        [w_in[:, :, c0:c1], w_in[:, :, 0:c0]] + cq + [w_in[:, :, c2 + W_ATT:], w_in[:, :, c1:c2]], axis=-1)
    w_in_p = jnp.pad(w_in_p, ((0, 0), (0, 0), (0, IN_PAD - w_in_p.shape[-1]))).astype(BF16)

    w_out_a = w_out[:, 0:W_ATT].astype(BF16)
    w_out_b = w_out[:, W_ATT:W_ATT + W_GDN].astype(BF16)
    wc = w_out[:, W_ATT + W_GDN:]
    w_out_c = jnp.concatenate([wc[:, h * HEAD_DIM:(h + 1) * HEAD_DIM] for h in C_HEAD_ORDER], axis=1).astype(BF16)
    w_gu = w_gate_up.astype(BF16)
    w_dn = w_down.astype(BF16)

    tile4 = lambda v: jnp.tile(v.reshape(1, -1), (1, N_HEADS))

    xt = x.reshape(t, d)
    for i in range(depth):
        proj, proj_ab, (qa_cls, ka_cls, va_cls), (qa, ka, va, qc, kc, vc) = _in_proj(
            xt, norm1[i].reshape(1, d), w_in_p[i], tile4(qn_a[i]), tile4(kn_a[i]), tile4(qn_c[i]), tile4(kn_c[i]),
            tab_a, tab_c, e_mat, batch, seq)
        o1, lse1 = _attn_a_local(qa, ka, va, mult_local, batch, seq)
        o_a = _attn_a_dilated(qa_cls, ka_cls, va_cls, mult_dil, o1, lse1, batch, seq)
        o_c = _attn_c(qc, kc, vc, batch, seq)
        qkv_b = _gdn_prep(proj, conv_b[i], batch, seq)
        alog, dtb = _pad_lanes(a_log_b[i], 128), _pad_lanes(dt_bias_b[i], 128)
        o_f, o_r = _gdn(qkv_b, proj_ab, alog, dtb, batch, seq)
        xt = _out_proj_ffn(xt, o_a, o_f, o_r, proj, o_c, onorm_b[i].reshape(1, B_DK),
                           w_out_a[i], w_out_b[i], w_out_c[i], norm2[i].reshape(1, d), w_gu[i], w_dn[i])
    return xt.reshape(batch, seq, d)
```

```python
import functools

import numpy as np
import jax
import jax.numpy as jnp
from jax import lax
from jax.experimental import pallas as pl
from jax.experimental.pallas import tpu as pltpu

F32 = jnp.float32
BF16 = jnp.bfloat16

EPS = 1e-6
GRID_W = 64
HEAD_DIM = 64
N_HEADS = 4
A_PATTERNS = ((128, 1), (512, 4), (2048, 16))
A_ROT_HALF = 8
A_THETA = 500000.0
C_ROT_HALF = 16
C_THETA = 10000.0
B_DK = 128
B_CONV = 5
CHUNK = 64
NEG = -1e30

W_ATT = N_HEADS * HEAD_DIM
W_GDN = N_HEADS * B_DK
COL_BQ, COL_BK, COL_BV, COL_BZ = 0, 512, 1024, 1536
COL_AQ, COL_AK, COL_AV = 2048, 2304, 2560
COL_CQ, COL_CK, COL_CV = 2816, 3072, 3200
COL_AB = 3328
IN_PAD = 3456
N_CLS = 4
A_LOCAL_TQ = 128
C_HEAD_ORDER = (0, 2, 1, 3)

VMEM_LIMIT = 56 * 1024 * 1024


def _cparams(sem):
    return pltpu.CompilerParams(dimension_semantics=sem, vmem_limit_bytes=VMEM_LIMIT)


def _split2(x):
    hi = x.astype(BF16)
    lo = (x - hi.astype(F32)).astype(BF16)
    return hi, lo


def _dot(a, b):
    return jnp.dot(a, b, preferred_element_type=F32)


def _dot_nt(a, b):
    return lax.dot_general(a, b, (((1,), (1,)), ((), ())), preferred_element_type=F32)


def _dot_tn(a, b):
    return lax.dot_general(a, b, (((0,), (0,)), ((), ())), preferred_element_type=F32)


def _sigmoid(x):
    return 1.0 / (1.0 + jnp.exp(-x))


def _silu(x):
    hx = 0.5 * x
    return hx + hx * jnp.tanh(hx)


def _head_norm_rope(x, w, e_hi, cos, sin_lo, sin_hi, rot_half, scale):
    xx = x * x
    hi, lo = _split2(xx)
    ms = _dot(hi, e_hi) + _dot(lo, e_hi)
    y = (x * lax.rsqrt(ms + EPS)) * w
    width = x.shape[1]
    up = pltpu.roll(y, width - rot_half, 1)
    dn = pltpu.roll(y, rot_half, 1)
    out = y * cos + up * sin_lo + dn * sin_hi
    return out * scale


def _join_halves(halves):
    return jnp.concatenate(halves, axis=1)


def _inproj_kernel(x_ref, nw_ref, w_ref, qna_ref, kna_ref, qnc_ref, knc_ref, ta_ref, tc_ref, e_ref,
                   ob_ref, oab_ref, oqa_cls_ref, oka_cls_ref, ova_cls_ref, oqa_ref, oka_ref, ova_ref,
                   oqc_ref, okc_ref, ovc_ref, scr_ref):
    tm = x_ref.shape[0]
    nh = 2
    hr = tm // nh
    rsl = [slice(h * hr, (h + 1) * hr) for h in range(nh)]
    ys = []
    for r in rsl:
        x = x_ref[r, :]
        ms = jnp.mean(x * x, axis=-1, keepdims=True)
        ys.append(((x * lax.rsqrt(ms + EPS)) * nw_ref[...]).astype(BF16))
    ress = [_dot(y, w_ref[...]) for y in ys]

    e = e_ref[...]
    scale = HEAD_DIM ** -0.5
    crows = hr // N_CLS
    for h, (r, res) in enumerate(zip(rsl, ress)):
        ob_ref[r, :] = res[:, COL_BQ:COL_AQ].astype(BF16)
        oab_ref[r, :] = res[:, COL_AB:IN_PAD]
        cos_a, sl_a, sh_a = ta_ref[0, r, :], ta_ref[1, r, :], ta_ref[2, r, :]
        cos_c, sl_c, sh_c = tc_ref[0, r, :], tc_ref[1, r, :], tc_ref[2, r, :]

        def emit(v, tok_ref, cls_ref):
            tok_ref[r, :] = v.astype(BF16)
            for j in range(2):
                scr_ref[h, j] = v[:, j * 128:(j + 1) * 128]
            for c in range(N_CLS):
                grouped = _join_halves([scr_ref[h, j, pl.ds(c, crows, stride=N_CLS), :] for j in range(2)])
                cls_ref[c, h * crows:(h + 1) * crows, :] = grouped.astype(BF16)

        col = lambda c0, w: res[:, c0:c0 + w]
        emit(_head_norm_rope(col(COL_AQ, 256), qna_ref[...], e, cos_a, sl_a, sh_a, A_ROT_HALF, scale), oqa_ref, oqa_cls_ref)
        emit(_head_norm_rope(col(COL_AK, 256), kna_ref[...], e, cos_a, sl_a, sh_a, A_ROT_HALF, 1.0), oka_ref, oka_cls_ref)
        emit(col(COL_AV, 256), ova_ref, ova_cls_ref)
        oqc_ref[r, :] = _head_norm_rope(col(COL_CQ, 256), qnc_ref[...], e, cos_c, sl_c, sh_c, C_ROT_HALF, scale).astype(BF16)
        okc_ref[r, :] = _head_norm_rope(col(COL_CK, 128), knc_ref[...][:, :128], e[:128, :128],
                                        cos_c[:, :128], sl_c[:, :128], sh_c[:, :128], C_ROT_HALF, 1.0).astype(BF16)
        ovc_ref[r, :] = col(COL_CV, 128).astype(BF16)


def _in_proj(x, nw, w, qna, kna, qnc, knc, tab_a, tab_c, e_mat, batch, seq, tm=512):
    t, d = x.shape
    n = w.shape[1]
    ns = seq // tm
    vec = pl.BlockSpec((1, W_ATT), lambda i: (0, 0))
    tab = pl.BlockSpec((3, tm, W_ATT), lambda i: (0, i % ns, 0))
    ob = lambda wd: pl.BlockSpec((tm, wd), lambda i: (i, 0))
    cls = pl.BlockSpec((None, N_CLS, tm // N_CLS, W_ATT), lambda i: (i // ns, 0, i % ns, 0))
    widths = (256, 256, 256, 256, 128, 128)
    outs = pl.pallas_call(
        _inproj_kernel,
        grid=(t // tm,),
        in_specs=[pl.BlockSpec((tm, d), lambda i: (i, 0)),
                  pl.BlockSpec((1, d), lambda i: (0, 0)),
                  pl.BlockSpec((d, n), lambda i: (0, 0)),
                  vec, vec, vec, vec, tab, tab,
                  pl.BlockSpec((W_ATT, W_ATT), lambda i: (0, 0))],
        out_specs=[ob(COL_AQ), ob(n - COL_AB), cls, cls, cls] + [ob(wd) for wd in widths],
        out_shape=[jax.ShapeDtypeStruct((t, COL_AQ), BF16), jax.ShapeDtypeStruct((t, n - COL_AB), F32)]
                  + [jax.ShapeDtypeStruct((batch, N_CLS, seq // N_CLS, W_ATT), BF16)] * 3
                  + [jax.ShapeDtypeStruct((t, wd), BF16) for wd in widths],
        scratch_shapes=[pltpu.VMEM((2, 2, tm // 2, 128), F32)],
        compiler_params=_cparams(("parallel",)),
        name="in_proj",
    )(x, nw, w, qna, kna, qnc, knc, tab_a, tab_c, e_mat)
    return outs[0], outs[1], outs[2:5], outs[5:]


def _head_masks(width):
    lane = lax.broadcasted_iota(jnp.int32, (1, width), 1)
    return [(lane >= h * HEAD_DIM) & (lane < (h + 1) * HEAD_DIM) for h in range(width // HEAD_DIM)]


def _attn_a_local_kernel(q_ref, k_ref, v_ref, m_ref, o1_ref, lse_ref):
    i = pl.program_id(1)
    seq = k_ref.shape[0]
    tq, win = m_ref.shape[1], m_ref.shape[2]
    nblk = q_ref.shape[0] // tq
    last = seq // tq - 1
    hms = _head_masks(W_ATT)
    vs, scores = [], []
    for j in range(nblk):
        blk = i * nblk + j
        start = jnp.clip(blk * tq - (win - tq) // 2, 0, seq - win)
        rows = pl.ds(pl.multiple_of(start, 64), win)
        k = k_ref[rows, :]
        vs.append(v_ref[rows, :])
        mult = m_ref[1]
        if j == 0:
            mult = jnp.where(blk == 0, m_ref[0], mult)
        if j == nblk - 1:
            mult = jnp.where(blk == last, m_ref[2], mult)
        q = q_ref[j * tq:(j + 1) * tq, :]
        scores.append([jnp.where(mult > 0, _dot_nt(jnp.where(hm, q, jnp.zeros_like(q)), k), NEG) for hm in hms])
    maxes = [[jnp.max(s, axis=-1, keepdims=True) for s in row] for row in scores]
    probs = [[jnp.exp(s - m) for s, m in zip(srow, mrow)] for srow, mrow in zip(scores, maxes)]
    sums = [[jnp.sum(p, axis=-1, keepdims=True) for p in row] for row in probs]
    pvs = [[_dot(p.astype(BF16), v) for p in row] for row, v in zip(probs, vs)]
    for j in range(nblk):
        out = jnp.zeros((tq, W_ATT), F32)
        lse = jnp.zeros((tq, W_ATT), F32)
        for h in range(N_HEADS):
            out = jnp.where(hms[h], pvs[j][h] * (1.0 / sums[j][h]), out)
            lse = jnp.where(hms[h], maxes[j][h] + jnp.log(sums[j][h]), lse)
        for half in range(2):
            o1_ref[half, j * tq:(j + 1) * tq, :] = out[:, half * 128:(half + 1) * 128]
            lse_ref[half, j * tq:(j + 1) * tq, :] = lse[:, half * 128:(half + 1) * 128]


def _attn_a_local(q, k, v, mult, batch, seq, nblk=8):
    tq = mult.shape[1]
    rows = nblk * tq
    nq = seq // rows
    return pl.pallas_call(
        _attn_a_local_kernel,
        grid=(batch, nq),
        in_specs=[pl.BlockSpec((rows, W_ATT), lambda b, i: (b * nq + i, 0)),
                  pl.BlockSpec((seq, W_ATT), lambda b, i: (b, 0)),
                  pl.BlockSpec((seq, W_ATT), lambda b, i: (b, 0)),
                  pl.BlockSpec(mult.shape, lambda b, i: (0, 0, 0))],
        out_specs=[pl.BlockSpec((2, rows, 128), lambda b, i: (0, b * nq + i, 0)),
                   pl.BlockSpec((2, rows, 128), lambda b, i: (0, b * nq + i, 0))],
        out_shape=[jax.ShapeDtypeStruct((2, batch * seq, 128), F32), jax.ShapeDtypeStruct((2, batch * seq, 128), F32)],
        compiler_params=_cparams(("parallel", "parallel")),
        name="attn_a_local",
    )(q, k, v, mult)


def _attn_a_dilated_kernel(q_ref, k_ref, v_ref, m_ref, o1_ref, lse1_ref, o_ref):
    tq = q_ref.shape[1]
    mult = m_ref[...]
    multf = mult.astype(F32)
    hms = _head_masks(W_ATT)
    qs = [q_ref[c] for c in range(N_CLS)]
    scores = [[jnp.where(mult > 0, _dot_nt(jnp.where(hm, q, jnp.zeros_like(q)), k_ref[c]), NEG) for hm in hms]
              for c, q in enumerate(qs)]
    m2 = [[jnp.max(s, axis=-1, keepdims=True) for s in row] for row in scores]
    probs = [[jnp.exp(s - m) * multf for s, m in zip(srow, mrow)] for srow, mrow in zip(scores, m2)]
    l2 = [[jnp.sum(p, axis=-1, keepdims=True) for p in row] for row in probs]
    pv2 = [[_dot(p.astype(BF16), v_ref[c]) for p in row] for c, row in enumerate(probs)]
    for c in range(N_CLS):
        m2b = jnp.zeros((tq, W_ATT), F32)
        l2b = jnp.zeros((tq, W_ATT), F32)
        pv2b = jnp.zeros((tq, W_ATT), F32)
        for h in range(N_HEADS):
            m2b = jnp.where(hms[h], m2[c][h], m2b)
            l2b = jnp.where(hms[h], l2[c][h], l2b)
            pv2b = jnp.where(hms[h], pv2[c][h], pv2b)
        tok = pl.ds(c, tq, stride=N_CLS)
        lse1 = _join_halves([lse1_ref[half, tok, :] for half in range(2)])
        o1 = _join_halves([o1_ref[half, tok, :] for half in range(2)])
        m = jnp.maximum(lse1, m2b)
        a1 = jnp.exp(lse1 - m)
        a2 = jnp.exp(m2b - m)
        out = (o1 * a1 + pv2b * a2) / (a1 + l2b * a2)
        for half in range(2):
            o_ref[half, tok, :] = out[:, half * 128:(half + 1) * 128]


def _attn_a_dilated(q, k, v, mult, o1, lse1, batch, seq, tq=256):
    rows = seq // N_CLS
    nu = rows // tq
    tok = pl.BlockSpec((2, N_CLS * tq, 128), lambda u, b: (0, b * nu + u, 0))
    return pl.pallas_call(
        _attn_a_dilated_kernel,
        grid=(nu, batch),
        in_specs=[pl.BlockSpec((None, N_CLS, tq, W_ATT), lambda u, b: (b, 0, u, 0)),
                  pl.BlockSpec((None, N_CLS, rows, W_ATT), lambda u, b: (b, 0, 0, 0)),
                  pl.BlockSpec((None, N_CLS, rows, W_ATT), lambda u, b: (b, 0, 0, 0)),
                  pl.BlockSpec((tq, rows), lambda u, b: (u, 0)),
                  tok, tok],
        out_specs=tok,
        out_shape=jax.ShapeDtypeStruct((2, batch * seq, 128), F32),
        compiler_params=_cparams(("parallel", "parallel")),
        name="attn_a_dilated",
    )(q, k, v, mult, o1, lse1)


def _attn_c_kernel(q_ref, k_ref, v_ref, o_ref):
    k = k_ref[...]
    v = v_ref[...]
    hms = _head_masks(128)
    qs = [q_ref[:, pair * 128:(pair + 1) * 128] for pair in range(2)]
    scores = [[_dot_nt(jnp.where(hm, q, jnp.zeros_like(q)), k) for hm in hms] for q in qs]
    maxes = [[jnp.max(s, axis=-1, keepdims=True) for s in row] for row in scores]
    probs = [[jnp.exp(s - m) for s, m in zip(srow, mrow)] for srow, mrow in zip(scores, maxes)]
    sums = [[jnp.sum(p, axis=-1, keepdims=True) for p in row] for row in probs]
    pvs = [[_dot(p.astype(BF16), v) for p in row] for row in probs]
    for pair in range(2):
        acc = jnp.zeros(qs[pair].shape, F32)
        for half in range(2):
            acc = jnp.where(hms[half], pvs[pair][half] * (1.0 / sums[pair][half]), acc)
        o_ref[:, pair * 128:(pair + 1) * 128] = acc.astype(o_ref.dtype)


def _attn_c(q, k, v, batch, seq, tq=512):
    nq = seq // tq
    return pl.pallas_call(
        _attn_c_kernel,
        grid=(batch, nq),
        in_specs=[pl.BlockSpec((tq, W_ATT), lambda b, i: (b * nq + i, 0)),
                  pl.BlockSpec((seq, 128), lambda b, i: (b, 0)),
                  pl.BlockSpec((seq, 128), lambda b, i: (b, 0))],
        out_specs=pl.BlockSpec((tq, W_ATT), lambda b, i: (b * nq + i, 0)),
        out_shape=jax.ShapeDtypeStruct((batch * seq, W_ATT), BF16),
        compiler_params=_cparams(("parallel", "parallel")),
        name="attn_c",
    )(q, k, v)


def _gdn_prep_kernel(x_ref, w_ref, o_ref):
    j = pl.program_id(1)
    x = x_ref[...].astype(F32)
    w = w_ref[...]
    seq = x.shape[0]
    pos = lax.broadcasted_iota(jnp.int32, (seq, 1), 0)
    acc = x * w[B_CONV // 2:B_CONV // 2 + 1, :]
    for tap in range(B_CONV):
        off = tap - B_CONV // 2
        if off == 0:
            continue
        sh = pltpu.roll(x, (-off) % seq, 0)
        ok = (pos + off >= 0) & (pos + off < seq)
        acc = acc + jnp.where(ok, sh, 0.0) * w[tap:tap + 1, :]
    y = _silu(acc)
    norm_scale = jnp.where(j == 0, B_DK ** -0.5, 1.0).astype(F32)
    for h in range(N_HEADS):
        yh = y[:, h * B_DK:(h + 1) * B_DK]
        ss = jnp.sum(yh * yh, axis=-1, keepdims=True)
        nh = yh * (lax.rsqrt(ss + EPS) * norm_scale)
        o_ref[:, h * B_DK:(h + 1) * B_DK] = jnp.where(j < 2, nh, yh).astype(o_ref.dtype)


def _gdn_prep(proj, conv_w, batch, seq):
    return pl.pallas_call(
        _gdn_prep_kernel,
        grid=(batch, 3),
        in_specs=[pl.BlockSpec((seq, W_GDN), lambda b, j: (b, j)),
                  pl.BlockSpec((B_CONV, W_GDN), lambda b, j: (0, j))],
        out_specs=pl.BlockSpec((seq, W_GDN), lambda b, j: (b, j)),
        out_shape=jax.ShapeDtypeStruct((batch * seq, 3 * W_GDN), BF16),
        compiler_params=_cparams(("parallel", "parallel")),
        name="gdn_prep",
    )(proj, conv_w)


def _lane_block(width, block, h):
    lane = lax.broadcasted_iota(jnp.int32, (1, width), 1)
    return (lane >= h * block) & (lane < (h + 1) * block)


def _block_diag(x, block):
    width = x.shape[1]
    zero = jnp.zeros_like(x)
    return jnp.concatenate([jnp.where(_lane_block(width, block, h), x, zero) for h in range(N_HEADS)], axis=0)


def _unit_tri_inverse_cat(a_list, eye_t):
    p = [eye_t - a for a in a_list]
    ab = [a.astype(BF16) for a in a_list]
    ap = [_dot(x, _block_diag(x, CHUNK)) for x in ab]
    for lvl in range(5):
        apb = [x.astype(BF16) for x in ap]
        bd = [_block_diag(x, CHUNK) for x in apb]
        if lvl < 4:
            r = [_dot(jnp.concatenate([pi.astype(BF16), xi], axis=0), bi) for pi, xi, bi in zip(p, apb, bd)]
            p = [pi + ri[0:CHUNK] for pi, ri in zip(p, r)]
            ap = [ri[CHUNK:2 * CHUNK] for ri in r]
        else:
            p = [pi + _dot(pi.astype(BF16), bi) for pi, bi in zip(p, bd)]
    return p


def _chunk_cumsum(g, rev):
    ridx = lax.broadcasted_iota(jnp.int32, g.shape, 0)
    x = g
    s = 1
    while s < CHUNK:
        if rev:
            sh = jnp.where(ridx < CHUNK - s, pltpu.roll(x, CHUNK - s, 0), 0.0)
        else:
            sh = jnp.where(ridx >= s, pltpu.roll(x, s, 0), 0.0)
        x = x + sh
        s *= 2
    return x


def _gdn_chunk_operands(qkv_ref, ab_ref, alog, dtb, rows, rev):
    c0 = N_HEADS if rev else 0
    ab = ab_ref[rows, :]
    xg = ab + dtb
    softplus = jnp.maximum(xg, 0.0) + jnp.log(1.0 + jnp.exp(-jnp.abs(xg)))
    g = -jnp.exp(alog) * softplus
    beta = _sigmoid(ab)
    gc = _chunk_cumsum(g, rev)
    g_last = gc[0:1, :] if rev else gc[CHUNK - 1:CHUNK, :]
    eg = jnp.exp(gc)
    e_rest = jnp.exp(g_last - gc)

    wcat = N_HEADS * CHUNK
    ii = lax.broadcasted_iota(jnp.int32, (CHUNK, 1), 0)
    jl = lax.broadcasted_iota(jnp.int32, (1, wcat), 1) & (CHUNK - 1)
    eye_b = ii == jl
    incl = (ii <= jl) if rev else (ii >= jl)
    g_col = jnp.zeros((CHUNK, wcat), F32)
    for h in range(N_HEADS):
        g_col = jnp.where(_lane_block(wcat, CHUNK, h), gc[:, c0 + h:c0 + h + 1], g_col)
    g_row = jnp.sum(jnp.where(eye_b, g_col, 0.0), axis=0, keepdims=True)
    dec = jnp.exp(jnp.where(incl, g_col - g_row, NEG))

    kb, vb, kbg, qg, kd = [], [], [], [], []
    for h in range(N_HEADS):
        c = c0 + h
        qh = qkv_ref[rows, h * B_DK:(h + 1) * B_DK].astype(F32)
        kh = qkv_ref[rows, W_GDN + h * B_DK:W_GDN + (h + 1) * B_DK].astype(F32)
        vh = qkv_ref[rows, 2 * W_GDN + h * B_DK:2 * W_GDN + (h + 1) * B_DK].astype(F32)
        bcol = beta[:, 8 + c:9 + c]
        egc = eg[:, c:c + 1]
        kbh = kh * bcol
        kb.append(kbh)
        vb.append(vh * bcol)
        kbg.append(kbh * egc)
        qg.append(qh * egc)
        kd.append((kh * e_rest[:, c:c + 1]).astype(BF16))
    cat = lambda xs: jnp.concatenate(xs, axis=1)
    return dict(dec=dec, eye_b=eye_b, kb=cat(kb).astype(BF16), vb=cat(vb).astype(BF16), kbg=cat(kbg).astype(BF16),
                qg=cat(qg), kd=kd, eg_last=jnp.exp(g_last),
                k=qkv_ref[rows, W_GDN:2 * W_GDN], q=qkv_ref[rows, 0:W_GDN])


def _gdn_solve_chunks(ops):
    kq = [_dot_nt(jnp.concatenate([o["kb"], o["q"]], axis=0), _block_diag(o["k"], B_DK)) for o in ops]
    a_cat = [jnp.where(o["eye_b"], 0.0, x[0:CHUNK] * o["dec"]) for o, x in zip(ops, kq)]
    qk = [(x[CHUNK:2 * CHUNK] * o["dec"]).astype(BF16) for o, x in zip(ops, kq)]
    eye_t = jnp.where(ops[0]["eye_b"], 1.0, 0.0).astype(F32)
    t_cat = [t.astype(BF16) for t in _unit_tri_inverse_cat(a_cat, eye_t)]
    ub = [_dot(t, _block_diag(o["vb"], B_DK)).astype(BF16) for o, t in zip(ops, t_cat)]
    wb = [_dot(t, _block_diag(o["kbg"], B_DK)).astype(BF16) for o, t in zip(ops, t_cat)]
    q_eff = [o["qg"] - _dot(x, _block_diag(w, B_DK)) for o, x, w in zip(ops, qk, wb)]
    o_intra = [_dot(x, _block_diag(u, B_DK)) for x, u in zip(qk, ub)]
    mc = []
    for o, u, w in zip(ops, ub, wb):
        per_head = []
        for h in range(N_HEADS):
            cols = slice(h * B_DK, (h + 1) * B_DK)
            per_head.append(_dot_tn(o["kd"][h], jnp.concatenate([w[:, cols], u[:, cols]], axis=1)))
        mc.append(per_head)
    return mc, q_eff, o_intra


def _gdn_kernel(qkv_ref, ab_ref, alog_ref, dtb_ref, of_ref, ob_ref, s_ref, m_ref, c_ref, qe_ref, egl_ref, *, group):
    seq = qkv_ref.shape[0]
    nc = seq // CHUNK
    alog = alog_ref[...]
    dtb = dtb_ref[...]
    o_refs = (of_ref, ob_ref)

    def solve(i, carry):
        ns = [i * group + j for j in range(group)]
        rows = [pl.ds(pl.multiple_of(n * CHUNK, CHUNK), CHUNK) for n in ns]
        for d, rev in enumerate((False, True)):
            ops = [_gdn_chunk_operands(qkv_ref, ab_ref, alog, dtb, r, rev) for r in rows]
            mc, q_eff, o_intra = _gdn_solve_chunks(ops)
            for j, n in enumerate(ns):
                slot = (d * nc + n) * N_HEADS
                for h in range(N_HEADS):
                    m_ref[slot + h] = mc[j][h][:, 0:B_DK].astype(BF16)
                    c_ref[slot + h] = mc[j][h][:, B_DK:2 * B_DK].astype(BF16)
                qe_ref[d, rows[j], :] = q_eff[j].astype(BF16)
                o_refs[d][rows[j], :] = o_intra[j].astype(BF16)
                egl_ref[d * nc + n] = jnp.broadcast_to(ops[j]["eg_last"], (8, 128))
        return carry

    lax.fori_loop(0, nc // group, solve, 0)

    s_ref[...] = jnp.zeros(s_ref.shape, F32)

    def scan(i, carry):
        work = []
        for d, n in enumerate((i, nc - 1 - i)):
            rows = pl.ds(pl.multiple_of(n * CHUNK, CHUNK), CHUNK)
            egl = egl_ref[d * nc + n]
            slot = (d * nc + n) * N_HEADS
            for h in range(N_HEADS):
                cols = slice(h * B_DK, (h + 1) * B_DK)
                state = s_ref[d * N_HEADS + h]
                lhs = jnp.concatenate([m_ref[slot + h], qe_ref[d, rows, cols]], axis=0)
                work.append((d, h, rows, cols, state, egl[0:1, d * N_HEADS + h:d * N_HEADS + h + 1], slot,
                             _dot(lhs, state.astype(BF16))))
        new = []
        for d, h, rows, cols, state, decay, slot, r in work:
            new.append((state * decay - r[0:B_DK] + c_ref[slot + h].astype(F32),
                        o_refs[d][rows, cols].astype(F32) + r[B_DK:B_DK + CHUNK]))
        for (d, h, rows, cols, *_), (st, o) in zip(work, new):
            s_ref[d * N_HEADS + h] = st
            o_refs[d][rows, cols] = o.astype(BF16)
        return carry

    lax.fori_loop(0, nc, scan, 0)


def _gdn(qkv, ab, alog, dtb, batch, seq, group=8):
    nc = seq // CHUNK
    out = pl.BlockSpec((seq, W_GDN), lambda b: (b, 0))
    return pl.pallas_call(
        functools.partial(_gdn_kernel, group=group),
        grid=(batch,),
        in_specs=[pl.BlockSpec((seq, 3 * W_GDN), lambda b: (b, 0)),
                  pl.BlockSpec((seq, 128), lambda b: (b, 0)),
                  pl.BlockSpec((1, 128), lambda b: (0, 0)),
                  pl.BlockSpec((1, 128), lambda b: (0, 0))],
        out_specs=[out, out],
        out_shape=[jax.ShapeDtypeStruct((batch * seq, W_GDN), BF16)] * 2,
        scratch_shapes=[pltpu.VMEM((2 * N_HEADS, B_DK, B_DK), F32),
                        pltpu.VMEM((2 * nc * N_HEADS, B_DK, B_DK), BF16),
                        pltpu.VMEM((2 * nc * N_HEADS, B_DK, B_DK), BF16),
                        pltpu.VMEM((2, seq, W_GDN), BF16),
                        pltpu.VMEM((2 * nc, 8, 128), F32)],
        compiler_params=_cparams(("parallel",)),
        name="gdn",
    )(qkv, ab, alog, dtb)


def _ffn_kernel(x_ref, oa_ref, of_ref, ob_ref, z_ref, oc_ref, onorm_ref, wa_ref, wb_ref, wc_ref,
                nw_ref, wg_ref, wu_ref, wd_ref, o_ref, xn_ref, acc_ref):
    j = pl.program_id(1)

    @pl.when(j == 0)
    def _():
        x = x_ref[...] + _dot(_join_halves([oa_ref[0], oa_ref[1]]).astype(BF16), wa_ref[...])
        x = x + _dot(oc_ref[...], wc_ref[...])
        for h in range(N_HEADS):
            cols = slice(h * B_DK, (h + 1) * B_DK)
            o = of_ref[:, cols].astype(F32) + ob_ref[:, cols].astype(F32)
            ms = jnp.mean(o * o, axis=-1, keepdims=True)
            y = (o * lax.rsqrt(ms + EPS)) * onorm_ref[...]
            gated = (y * _silu(z_ref[:, cols].astype(F32))).astype(BF16)
            x = x + _dot(gated, wb_ref[h * B_DK:(h + 1) * B_DK, :])
        ms = jnp.mean(x * x, axis=-1, keepdims=True)
        xn_ref[...] = ((x * lax.rsqrt(ms + EPS)) * nw_ref[...]).astype(BF16)
        acc_ref[...] = x

    hr = xn_ref.shape[0] // 2
    rsl = [slice(h * hr, (h + 1) * hr) for h in range(2)]
    xns = [xn_ref[r, :] for r in rsl]
    gates = [_dot(xn, wg_ref[...]) for xn in xns]
    ups = [_dot(xn, wu_ref[...]) for xn in xns]
    hiddens = [(_silu(g) * u).astype(BF16) for g, u in zip(gates, ups)]
    downs = [_dot(hd, wd_ref[...]) for hd in hiddens]
    for r, dn in zip(rsl, downs):
        acc_ref[r, :] += dn

    @pl.when(j == pl.num_programs(1) - 1)
    def _():
        o_ref[...] = acc_ref[...]


def _out_proj_ffn(x, oa, o_f, o_b, proj_b, oc, onorm, wa, wb, wc, nw, w_gate_up, w_down, tm=512, tf=2816):
    t, d = x.shape
    dff = w_down.shape[0]
    nf = dff // tf
    row = lambda w: pl.BlockSpec((tm, w), lambda i, j: (i, 0))
    full = lambda w: pl.BlockSpec((w, d), lambda i, j: (0, 0))
    return pl.pallas_call(
        _ffn_kernel,
        grid=(t // tm, nf),
        in_specs=[row(d), pl.BlockSpec((2, tm, 128), lambda i, j: (0, i, 0)), row(W_GDN), row(W_GDN),
                  pl.BlockSpec((tm, W_GDN), lambda i, j: (i, COL_BZ // W_GDN)),
                  row(W_ATT), pl.BlockSpec((1, B_DK), lambda i, j: (0, 0)),
                  full(W_ATT), full(W_GDN), full(W_ATT),
                  pl.BlockSpec((1, d), lambda i, j: (0, 0)),
                  pl.BlockSpec((d, tf), lambda i, j: (0, j), pipeline_mode=pl.Buffered(1)),
                  pl.BlockSpec((d, tf), lambda i, j: (0, j + nf), pipeline_mode=pl.Buffered(1)),
                  pl.BlockSpec((tf, d), lambda i, j: (j, 0), pipeline_mode=pl.Buffered(1))],
        out_specs=pl.BlockSpec((tm, d), lambda i, j: (i, 0)),
        out_shape=jax.ShapeDtypeStruct((t, d), F32),
        scratch_shapes=[pltpu.VMEM((tm, d), BF16), pltpu.VMEM((tm, d), F32)],
        compiler_params=_cparams(("parallel", "arbitrary")),
        name="ffn",
    )(x, oa, o_f, o_b, proj_b, oc, onorm, wa, wb, wc, nw, w_gate_up, w_gate_up, w_down)


def _rope_tables(seq):
    t = np.arange(seq, dtype=np.float64)

    def build(groups):
        cos = np.ones((seq, HEAD_DIM))
        s_lo = np.zeros((seq, HEAD_DIM))
        s_hi = np.zeros((seq, HEAD_DIM))
        for start, half, theta, pos in groups:
            inv = np.float64(np.float32(theta)) ** (-np.arange(half, dtype=np.float64) / half)
            ang = pos[:, None] * inv[None, :]
            cos[:, start:start + half] = np.cos(ang)
            cos[:, start + half:start + 2 * half] = np.cos(ang)
            s_lo[:, start:start + half] = -np.sin(ang)
            s_hi[:, start + half:start + 2 * half] = np.sin(ang)
        tab = np.stack([cos, s_lo, s_hi])
        return np.tile(tab, (1, 1, N_HEADS)).astype(np.float32)

    tab_a = build([(0, A_ROT_HALF, A_THETA, t)])
    row = np.floor(t / GRID_W)
    col = t - row * GRID_W
    tab_c = build([(0, C_ROT_HALF, C_THETA, row), (2 * C_ROT_HALF, C_ROT_HALF, C_THETA, col)])
    return tab_a, tab_c


def _multiplicity_tables(seq):
    (w0, d0), rest = A_PATTERNS[0], A_PATTERNS[1:]
    assert d0 == 1 and all(d % N_CLS == 0 for _, d in rest)
    r0 = w0 // 2
    tq, win = A_LOCAL_TQ, A_LOCAL_TQ + 2 * r0
    qi = np.arange(tq)[:, None]
    kj = np.arange(win)[None, :]
    local = np.stack([(np.abs(qi + off - kj) <= r0) for off in (0, r0, 2 * r0)]).astype(np.float32)
    n = seq // N_CLS
    d = np.arange(n)[:, None] - np.arange(n)[None, :]
    dilated = np.zeros((n, n), np.float32)
    for window, dil in rest:
        step = dil // N_CLS
        radius = window // (2 * dil)
        dilated += ((d % step == 0) & (np.abs(d) <= radius * step)).astype(np.float32)
    return local, dilated


def _head_mean_matrix():
    e = np.kron(np.eye(N_HEADS), np.full((HEAD_DIM, HEAD_DIM), 1.0 / HEAD_DIM))
    return e.astype(np.float32)


def _pad_lanes(v, width):
    return jnp.pad(v.reshape(1, -1), ((0, 0), (0, width - v.size)))


def kernel(x, norm1, w_in, qn_a, kn_a, conv_b, a_log_b, dt_bias_b, onorm_b, qn_c, kn_c, w_out, norm2, w_gate_up, w_down):
    batch, seq, d = x.shape
    depth = w_in.shape[0]
    t = batch * seq

    tab_a, tab_c = _rope_tables(seq)
    tab_a, tab_c = jnp.asarray(tab_a), jnp.asarray(tab_c)
    mult_local, mult_dil = _multiplicity_tables(seq)
    mult_local, mult_dil = jnp.asarray(mult_local, dtype=BF16), jnp.asarray(mult_dil, dtype=BF16)
    e_mat = jnp.asarray(_head_mean_matrix(), dtype=BF16)

    c0 = 3 * W_ATT
    c1 = c0 + 4 * W_GDN
    c2 = c1 + 4 * N_HEADS
    w_in_h = w_in.astype(BF16)
    cq = [w_in_h[:, :, c2 + h * HEAD_DIM:c2 + (h + 1) * HEAD_DIM] for h in C_HEAD_ORDER]
    w_in_p = jnp.concatenate(
        [w_in_h[:, :, c0:c1], w_in_h[:, :, 0:c0]] + cq + [w_in_h[:, :, c2 + W_ATT:], w_in_h[:, :, c1:c2],
         jnp.zeros(w_in.shape[:2] + (IN_PAD - w_in.shape[2],), BF16)], axis=-1)

    w_out_h = w_out.astype(BF16)
    w_out_a = w_out_h[:, 0:W_ATT]
    w_out_b = w_out_h[:, W_ATT:W_ATT + W_GDN]
    wc = w_out_h[:, W_ATT + W_GDN:]
    w_out_c = jnp.concatenate([wc[:, h * HEAD_DIM:(h + 1) * HEAD_DIM] for h in C_HEAD_ORDER], axis=1)
    w_gu = w_gate_up.astype(BF16)
    w_dn = w_down.astype(BF16)

    tile4 = lambda v: jnp.tile(v.reshape(1, -1), (1, N_HEADS))

    xt = x.reshape(t, d)
    for i in range(depth):
        proj, proj_ab, (qa_cls, ka_cls, va_cls), (qa, ka, va, qc, kc, vc) = _in_proj(
            xt, norm1[i].reshape(1, d), w_in_p[i], tile4(qn_a[i]), tile4(kn_a[i]), tile4(qn_c[i]), tile4(kn_c[i]),
            tab_a, tab_c, e_mat, batch, seq)
        o1, lse1 = _attn_a_local(qa, ka, va, mult_local, batch, seq)
        o_a = _attn_a_dilated(qa_cls, ka_cls, va_cls, mult_dil, o1, lse1, batch, seq)
        o_c = _attn_c(qc, kc, vc, batch, seq)
        qkv_b = _gdn_prep(proj, conv_b[i], batch, seq)
        alog, dtb = _pad_lanes(a_log_b[i], 128), _pad_lanes(dt_bias_b[i], 128)
        o_f, o_r = _gdn(qkv_b, proj_ab, alog, dtb, batch, seq)
        xt = _out_proj_ffn(xt, o_a, o_f, o_r, proj, o_c, onorm_b[i].reshape(1, B_DK),
                           w_out_a[i], w_out_b[i], w_out_c[i], norm2[i].reshape(1, d), w_gu[i], w_dn[i])
    return xt.reshape(batch, seq, d)
```

```python
import functools

import numpy as np
import jax
import jax.numpy as jnp
from jax import lax
from jax.experimental import pallas as pl
from jax.experimental.pallas import tpu as pltpu

F32 = jnp.float32
BF16 = jnp.bfloat16

EPS = 1e-6
GRID_W = 64
HEAD_DIM = 64
N_HEADS = 4
A_PATTERNS = ((128, 1), (512, 4), (2048, 16))
A_ROT_HALF = 8
A_THETA = 500000.0
C_ROT_HALF = 16
C_THETA = 10000.0
B_DK = 128
B_CONV = 5
CHUNK = 64
NEG = -1e30

W_ATT = N_HEADS * HEAD_DIM
W_GDN = N_HEADS * B_DK
COL_BQ, COL_BK, COL_BV, COL_BZ = 0, 512, 1024, 1536
COL_AQ, COL_AK, COL_AV = 2048, 2304, 2560
COL_CQ, COL_CK, COL_CV = 2816, 3072, 3200
COL_AB = 3328
IN_PAD = 3456
N_CLS = 4
A_LOCAL_TQ = 128
C_HEAD_ORDER = (0, 2, 1, 3)

VMEM_LIMIT = 56 * 1024 * 1024


def _cparams(sem):
    return pltpu.CompilerParams(dimension_semantics=sem, vmem_limit_bytes=VMEM_LIMIT)


def _split2(x):
    hi = x.astype(BF16)
    lo = (x - hi.astype(F32)).astype(BF16)
    return hi, lo


def _dot(a, b):
    return jnp.dot(a, b, preferred_element_type=F32)


def _dot_nt(a, b):
    return lax.dot_general(a, b, (((1,), (1,)), ((), ())), preferred_element_type=F32)


def _dot_tn(a, b):
    return lax.dot_general(a, b, (((0,), (0,)), ((), ())), preferred_element_type=F32)


def _sigmoid(x):
    return 1.0 / (1.0 + jnp.exp(-x))


def _silu(x):
    hx = 0.5 * x
    return hx + hx * jnp.tanh(hx)


def _head_norm_rope(x, w, e_hi, cos, sin_lo, sin_hi, rot_half, scale):
    xx = x * x
    hi, lo = _split2(xx)
    ms = _dot(hi, e_hi) + _dot(lo, e_hi)
    y = (x * lax.rsqrt(ms + EPS)) * w
    width = x.shape[1]
    up = pltpu.roll(y, width - rot_half, 1)
    dn = pltpu.roll(y, rot_half, 1)
    out = y * cos + up * sin_lo + dn * sin_hi
    return out * scale


def _join_halves(halves):
    return jnp.concatenate(halves, axis=1)


def _inproj_kernel(x_ref, nw_ref, w_ref, qna_ref, kna_ref, qnc_ref, knc_ref, ta_ref, tc_ref, e_ref,
                   ob_ref, oab_ref, oqa_cls_ref, oka_cls_ref, ova_cls_ref, oqa_ref, oka_ref, ova_ref,
                   oqc_ref, okc_ref, ovc_ref, scr_ref):
    tm = x_ref.shape[0]
    nh = 2
    hr = tm // nh
    rsl = [slice(h * hr, (h + 1) * hr) for h in range(nh)]
    ys = []
    for r in rsl:
        x = x_ref[r, :]
        ms = jnp.mean(x * x, axis=-1, keepdims=True)
        ys.append(((x * lax.rsqrt(ms + EPS)) * nw_ref[...]).astype(BF16))
    ress = [_dot(y, w_ref[...]) for y in ys]

    e = e_ref[...]
    scale = HEAD_DIM ** -0.5
    crows = hr // N_CLS
    for h, (r, res) in enumerate(zip(rsl, ress)):
        ob_ref[r, :] = res[:, COL_BQ:COL_AQ].astype(BF16)
        oab_ref[r, :] = res[:, COL_AB:IN_PAD]
        cos_a, sl_a, sh_a = ta_ref[0, r, :], ta_ref[1, r, :], ta_ref[2, r, :]
        cos_c, sl_c, sh_c = tc_ref[0, r, :], tc_ref[1, r, :], tc_ref[2, r, :]

        def emit(v, tok_ref, cls_ref):
            tok_ref[r, :] = v.astype(BF16)
            for j in range(2):
                scr_ref[h, j] = v[:, j * 128:(j + 1) * 128]
            for c in range(N_CLS):
                grouped = _join_halves([scr_ref[h, j, pl.ds(c, crows, stride=N_CLS), :] for j in range(2)])
                cls_ref[c, h * crows:(h + 1) * crows, :] = grouped.astype(BF16)

        col = lambda c0, w: res[:, c0:c0 + w]
        emit(_head_norm_rope(col(COL_AQ, 256), qna_ref[...], e, cos_a, sl_a, sh_a, A_ROT_HALF, scale), oqa_ref, oqa_cls_ref)
        emit(_head_norm_rope(col(COL_AK, 256), kna_ref[...], e, cos_a, sl_a, sh_a, A_ROT_HALF, 1.0), oka_ref, oka_cls_ref)
        emit(col(COL_AV, 256), ova_ref, ova_cls_ref)
        oqc_ref[r, :] = _head_norm_rope(col(COL_CQ, 256), qnc_ref[...], e, cos_c, sl_c, sh_c, C_ROT_HALF, scale).astype(BF16)
        okc_ref[r, :] = _head_norm_rope(col(COL_CK, 128), knc_ref[...][:, :128], e[:128, :128],
                                        cos_c[:, :128], sl_c[:, :128], sh_c[:, :128], C_ROT_HALF, 1.0).astype(BF16)
        ovc_ref[r, :] = col(COL_CV, 128).astype(BF16)


def _in_proj(x, nw, w, qna, kna, qnc, knc, tab_a, tab_c, e_mat, batch, seq, tm=512):
    t, d = x.shape
    n = w.shape[1]
    ns = seq // tm
    vec = pl.BlockSpec((1, W_ATT), lambda i: (0, 0))
    tab = pl.BlockSpec((3, tm, W_ATT), lambda i: (0, i % ns, 0))
    ob = lambda wd: pl.BlockSpec((tm, wd), lambda i: (i, 0))
    cls = pl.BlockSpec((None, N_CLS, tm // N_CLS, W_ATT), lambda i: (i // ns, 0, i % ns, 0))
    widths = (256, 256, 256, 256, 128, 128)
    outs = pl.pallas_call(
        _inproj_kernel,
        grid=(t // tm,),
        in_specs=[pl.BlockSpec((tm, d), lambda i: (i, 0)),
                  pl.BlockSpec((1, d), lambda i: (0, 0)),
                  pl.BlockSpec((d, n), lambda i: (0, 0)),
                  vec, vec, vec, vec, tab, tab,
                  pl.BlockSpec((W_ATT, W_ATT), lambda i: (0, 0))],
        out_specs=[ob(COL_AQ), ob(n - COL_AB), cls, cls, cls] + [ob(wd) for wd in widths],
        out_shape=[jax.ShapeDtypeStruct((t, COL_AQ), BF16), jax.ShapeDtypeStruct((t, n - COL_AB), F32)]
                  + [jax.ShapeDtypeStruct((batch, N_CLS, seq // N_CLS, W_ATT), BF16)] * 3
                  + [jax.ShapeDtypeStruct((t, wd), BF16) for wd in widths],
        scratch_shapes=[pltpu.VMEM((2, 2, tm // 2, 128), F32)],
        compiler_params=_cparams(("parallel",)),
        name="in_proj",
    )(x, nw, w, qna, kna, qnc, knc, tab_a, tab_c, e_mat)
    return outs[0], outs[1], outs[2:5], outs[5:]


def _head_masks(width):
    lane = lax.broadcasted_iota(jnp.int32, (1, width), 1)
    return [(lane >= h * HEAD_DIM) & (lane < (h + 1) * HEAD_DIM) for h in range(width // HEAD_DIM)]


def _attn_a_local_kernel(q_ref, k_ref, v_ref, m_ref, o1_ref, lse_ref):
    i = pl.program_id(1)
    seq = k_ref.shape[0]
    tq, win = m_ref.shape[1], m_ref.shape[2]
    nblk = q_ref.shape[0] // tq
    last = seq // tq - 1
    hms = _head_masks(W_ATT)
    vs, scores = [], []
    for j in range(nblk):
        blk = i * nblk + j
        start = jnp.clip(blk * tq - (win - tq) // 2, 0, seq - win)
        rows = pl.ds(pl.multiple_of(start, 64), win)
        k = k_ref[rows, :]
        vs.append(v_ref[rows, :])
        mult = m_ref[1]
        if j == 0:
            mult = jnp.where(blk == 0, m_ref[0], mult)
        if j == nblk - 1:
            mult = jnp.where(blk == last, m_ref[2], mult)
        q = q_ref[j * tq:(j + 1) * tq, :]
        scores.append([jnp.where(mult > 0, _dot_nt(jnp.where(hm, q, jnp.zeros_like(q)), k), NEG) for hm in hms])
    maxes = [[jnp.max(s, axis=-1, keepdims=True) for s in row] for row in scores]
    probs = [[jnp.exp(s - m) for s, m in zip(srow, mrow)] for srow, mrow in zip(scores, maxes)]
    sums = [[jnp.sum(p, axis=-1, keepdims=True) for p in row] for row in probs]
    pvs = [[_dot(p.astype(BF16), v) for p in row] for row, v in zip(probs, vs)]
    for j in range(nblk):
        out = jnp.zeros((tq, W_ATT), F32)
        lse = jnp.zeros((tq, W_ATT), F32)
        for h in range(N_HEADS):
            out = jnp.where(hms[h], pvs[j][h] * (1.0 / sums[j][h]), out)
            lse = jnp.where(hms[h], maxes[j][h] + jnp.log(sums[j][h]), lse)
        for half in range(2):
            o1_ref[half, j * tq:(j + 1) * tq, :] = out[:, half * 128:(half + 1) * 128]
            lse_ref[half, j * tq:(j + 1) * tq, :] = lse[:, half * 128:(half + 1) * 128]


def _attn_a_local(q, k, v, mult, batch, seq, nblk=16):
    tq = mult.shape[1]
    rows = nblk * tq
    nq = seq // rows
    return pl.pallas_call(
        _attn_a_local_kernel,
        grid=(batch, nq),
        in_specs=[pl.BlockSpec((rows, W_ATT), lambda b, i: (b * nq + i, 0)),
                  pl.BlockSpec((seq, W_ATT), lambda b, i: (b, 0)),
                  pl.BlockSpec((seq, W_ATT), lambda b, i: (b, 0)),
                  pl.BlockSpec(mult.shape, lambda b, i: (0, 0, 0))],
        out_specs=[pl.BlockSpec((2, rows, 128), lambda b, i: (0, b * nq + i, 0)),
                   pl.BlockSpec((2, rows, 128), lambda b, i: (0, b * nq + i, 0))],
        out_shape=[jax.ShapeDtypeStruct((2, batch * seq, 128), F32), jax.ShapeDtypeStruct((2, batch * seq, 128), F32)],
        compiler_params=_cparams(("parallel", "parallel")),
        name="attn_a_local",
    )(q, k, v, mult)


def _attn_a_dilated_kernel(q_ref, k_ref, v_ref, m_ref, o1_ref, lse1_ref, o_ref):
    tq = q_ref.shape[1]
    mult = m_ref[...]
    multf = mult.astype(F32)
    hms = _head_masks(W_ATT)
    qs = [q_ref[c] for c in range(N_CLS)]
    scores = [[jnp.where(mult > 0, _dot_nt(jnp.where(hm, q, jnp.zeros_like(q)), k_ref[c]), NEG) for hm in hms]
              for c, q in enumerate(qs)]
    m2 = [[jnp.max(s, axis=-1, keepdims=True) for s in row] for row in scores]
    probs = [[jnp.exp(s - m) * multf for s, m in zip(srow, mrow)] for srow, mrow in zip(scores, m2)]
    l2 = [[jnp.sum(p, axis=-1, keepdims=True) for p in row] for row in probs]
    pv2 = [[_dot(p.astype(BF16), v_ref[c]) for p in row] for c, row in enumerate(probs)]
    for c in range(N_CLS):
        m2b = jnp.zeros((tq, W_ATT), F32)
        l2b = jnp.zeros((tq, W_ATT), F32)
        pv2b = jnp.zeros((tq, W_ATT), F32)
        for h in range(N_HEADS):
            m2b = jnp.where(hms[h], m2[c][h], m2b)
            l2b = jnp.where(hms[h], l2[c][h], l2b)
            pv2b = jnp.where(hms[h], pv2[c][h], pv2b)
        tok = pl.ds(c, tq, stride=N_CLS)
        lse1 = _join_halves([lse1_ref[half, tok, :] for half in range(2)])
        o1 = _join_halves([o1_ref[half, tok, :] for half in range(2)])
        m = jnp.maximum(lse1, m2b)
        a1 = jnp.exp(lse1 - m)
        a2 = jnp.exp(m2b - m)
        out = (o1 * a1 + pv2b * a2) / (a1 + l2b * a2)
        for half in range(2):
            o_ref[half, tok, :] = out[:, half * 128:(half + 1) * 128]


def _attn_a_dilated(q, k, v, mult, o1, lse1, batch, seq, tq=512):
    rows = seq // N_CLS
    nu = rows // tq
    tok = pl.BlockSpec((2, N_CLS * tq, 128), lambda u, b: (0, b * nu + u, 0))
    return pl.pallas_call(
        _attn_a_dilated_kernel,
        grid=(nu, batch),
        in_specs=[pl.BlockSpec((None, N_CLS, tq, W_ATT), lambda u, b: (b, 0, u, 0)),
                  pl.BlockSpec((None, N_CLS, rows, W_ATT), lambda u, b: (b, 0, 0, 0)),
                  pl.BlockSpec((None, N_CLS, rows, W_ATT), lambda u, b: (b, 0, 0, 0)),
                  pl.BlockSpec((tq, rows), lambda u, b: (u, 0)),
                  tok, tok],
        out_specs=tok,
        out_shape=jax.ShapeDtypeStruct((2, batch * seq, 128), F32),
        compiler_params=_cparams(("parallel", "parallel")),
        name="attn_a_dilated",
    )(q, k, v, mult, o1, lse1)


def _attn_c_kernel(q_ref, k_ref, v_ref, o_ref):
    k = k_ref[...]
    v = v_ref[...]
    hms = _head_masks(128)
    qs = [q_ref[:, pair * 128:(pair + 1) * 128] for pair in range(2)]
    scores = [[_dot_nt(jnp.where(hm, q, jnp.zeros_like(q)), k) for hm in hms] for q in qs]
    maxes = [[jnp.max(s, axis=-1, keepdims=True) for s in row] for row in scores]
    probs = [[jnp.exp(s - m) for s, m in zip(srow, mrow)] for srow, mrow in zip(scores, maxes)]
    sums = [[jnp.sum(p, axis=-1, keepdims=True) for p in row] for row in probs]
    pvs = [[_dot(p.astype(BF16), v) for p in row] for row in probs]
    for pair in range(2):
        acc = jnp.zeros(qs[pair].shape, F32)
        for half in range(2):
            acc = jnp.where(hms[half], pvs[pair][half] * (1.0 / sums[pair][half]), acc)
        o_ref[:, pair * 128:(pair + 1) * 128] = acc.astype(o_ref.dtype)


def _attn_c(q, k, v, batch, seq, tq=512):
    nq = seq // tq
    return pl.pallas_call(
        _attn_c_kernel,
        grid=(batch, nq),
        in_specs=[pl.BlockSpec((tq, W_ATT), lambda b, i: (b * nq + i, 0)),
                  pl.BlockSpec((seq, 128), lambda b, i: (b, 0)),
                  pl.BlockSpec((seq, 128), lambda b, i: (b, 0))],
        out_specs=pl.BlockSpec((tq, W_ATT), lambda b, i: (b * nq + i, 0)),
        out_shape=jax.ShapeDtypeStruct((batch * seq, W_ATT), BF16),
        compiler_params=_cparams(("parallel", "parallel")),
        name="attn_c",
    )(q, k, v)


def _gdn_prep_kernel(x_ref, w_ref, o_ref):
    j = pl.program_id(1)
    x = x_ref[...].astype(F32)
    w = w_ref[...]
    seq = x.shape[0]
    pos = lax.broadcasted_iota(jnp.int32, (seq, 1), 0)
    acc = x * w[B_CONV // 2:B_CONV // 2 + 1, :]
    for tap in range(B_CONV):
        off = tap - B_CONV // 2
        if off == 0:
            continue
        sh = pltpu.roll(x, (-off) % seq, 0)
        ok = (pos + off >= 0) & (pos + off < seq)
        acc = acc + jnp.where(ok, sh, 0.0) * w[tap:tap + 1, :]
    y = _silu(acc)
    norm_scale = jnp.where(j == 0, B_DK ** -0.5, 1.0).astype(F32)
    for h in range(N_HEADS):
        yh = y[:, h * B_DK:(h + 1) * B_DK]
        ss = jnp.sum(yh * yh, axis=-1, keepdims=True)
        nh = yh * (lax.rsqrt(ss + EPS) * norm_scale)
        o_ref[:, h * B_DK:(h + 1) * B_DK] = jnp.where(j < 2, nh, yh).astype(o_ref.dtype)


def _gdn_prep(proj, conv_w, batch, seq):
    return pl.pallas_call(
        _gdn_prep_kernel,
        grid=(batch, 3),
        in_specs=[pl.BlockSpec((seq, W_GDN), lambda b, j: (b, j)),
                  pl.BlockSpec((B_CONV, W_GDN), lambda b, j: (0, j))],
        out_specs=pl.BlockSpec((seq, W_GDN), lambda b, j: (b, j)),
        out_shape=jax.ShapeDtypeStruct((batch * seq, 3 * W_GDN), BF16),
        compiler_params=_cparams(("parallel", "parallel")),
        name="gdn_prep",
    )(proj, conv_w)


def _lane_block(width, block, h):
    lane = lax.broadcasted_iota(jnp.int32, (1, width), 1)
    return (lane >= h * block) & (lane < (h + 1) * block)


def _block_diag(x, block):
    width = x.shape[1]
    zero = jnp.zeros_like(x)
    return jnp.concatenate([jnp.where(_lane_block(width, block, h), x, zero) for h in range(N_HEADS)], axis=0)


def _unit_tri_inverse_cat(a_list, eye_t):
    p = [eye_t - a for a in a_list]
    ab = [a.astype(BF16) for a in a_list]
    ap = [_dot(x, _block_diag(x, CHUNK)) for x in ab]
    for lvl in range(5):
        apb = [x.astype(BF16) for x in ap]
        bd = [_block_diag(x, CHUNK) for x in apb]
        if lvl < 4:
            r = [_dot(jnp.concatenate([pi.astype(BF16), xi], axis=0), bi) for pi, xi, bi in zip(p, apb, bd)]
            p = [pi + ri[0:CHUNK] for pi, ri in zip(p, r)]
            ap = [ri[CHUNK:2 * CHUNK] for ri in r]
        else:
            p = [pi + _dot(pi.astype(BF16), bi) for pi, bi in zip(p, bd)]
    return p


def _chunk_cumsum(g, rev):
    ridx = lax.broadcasted_iota(jnp.int32, g.shape, 0)
    x = g
    s = 1
    while s < CHUNK:
        if rev:
            sh = jnp.where(ridx < CHUNK - s, pltpu.roll(x, CHUNK - s, 0), 0.0)
        else:
            sh = jnp.where(ridx >= s, pltpu.roll(x, s, 0), 0.0)
        x = x + sh
        s *= 2
    return x


def _gdn_chunk_operands(qkv_ref, ab_ref, alog, dtb, rows, rev):
    c0 = N_HEADS if rev else 0
    ab = ab_ref[rows, :]
    xg = ab + dtb
    softplus = jnp.maximum(xg, 0.0) + jnp.log(1.0 + jnp.exp(-jnp.abs(xg)))
    g = -jnp.exp(alog) * softplus
    beta = _sigmoid(ab)
    gc = _chunk_cumsum(g, rev)
    g_last = gc[0:1, :] if rev else gc[CHUNK - 1:CHUNK, :]
    eg = jnp.exp(gc)
    e_rest = jnp.exp(g_last - gc)

    wcat = N_HEADS * CHUNK
    ii = lax.broadcasted_iota(jnp.int32, (CHUNK, 1), 0)
    jl = lax.broadcasted_iota(jnp.int32, (1, wcat), 1) & (CHUNK - 1)
    eye_b = ii == jl
    incl = (ii <= jl) if rev else (ii >= jl)
    g_col = jnp.zeros((CHUNK, wcat), F32)
    for h in range(N_HEADS):
        g_col = jnp.where(_lane_block(wcat, CHUNK, h), gc[:, c0 + h:c0 + h + 1], g_col)
    g_row = jnp.sum(jnp.where(eye_b, g_col, 0.0), axis=0, keepdims=True)
    dec = jnp.exp(jnp.where(incl, g_col - g_row, NEG))

    kb, vb, kbg, qg, kd = [], [], [], [], []
    for h in range(N_HEADS):
        c = c0 + h
        qh = qkv_ref[rows, h * B_DK:(h + 1) * B_DK].astype(F32)
        kh = qkv_ref[rows, W_GDN + h * B_DK:W_GDN + (h + 1) * B_DK].astype(F32)
        vh = qkv_ref[rows, 2 * W_GDN + h * B_DK:2 * W_GDN + (h + 1) * B_DK].astype(F32)
        bcol = beta[:, 8 + c:9 + c]
        egc = eg[:, c:c + 1]
        kbh = kh * bcol
        kb.append(kbh)
        vb.append(vh * bcol)
        kbg.append(kbh * egc)
        qg.append(qh * egc)
        kd.append((kh * e_rest[:, c:c + 1]).astype(BF16))
    cat = lambda xs: jnp.concatenate(xs, axis=1)
    return dict(dec=dec, eye_b=eye_b, kb=cat(kb).astype(BF16), vb=cat(vb).astype(BF16), kbg=cat(kbg).astype(BF16),
                qg=cat(qg), kd=kd, eg_last=jnp.exp(g_last),
                k=qkv_ref[rows, W_GDN:2 * W_GDN], q=qkv_ref[rows, 0:W_GDN])


def _gdn_solve_chunks(ops):
    kq = [_dot_nt(jnp.concatenate([o["kb"], o["q"]], axis=0), _block_diag(o["k"], B_DK)) for o in ops]
    a_cat = [jnp.where(o["eye_b"], 0.0, x[0:CHUNK] * o["dec"]) for o, x in zip(ops, kq)]
    qk = [(x[CHUNK:2 * CHUNK] * o["dec"]).astype(BF16) for o, x in zip(ops, kq)]
    eye_t = jnp.where(ops[0]["eye_b"], 1.0, 0.0).astype(F32)
    t_cat = [t.astype(BF16) for t in _unit_tri_inverse_cat(a_cat, eye_t)]
    ub = [_dot(t, _block_diag(o["vb"], B_DK)).astype(BF16) for o, t in zip(ops, t_cat)]
    wb = [_dot(t, _block_diag(o["kbg"], B_DK)).astype(BF16) for o, t in zip(ops, t_cat)]
    q_eff = [o["qg"] - _dot(x, _block_diag(w, B_DK)) for o, x, w in zip(ops, qk, wb)]
    o_intra = [_dot(x, _block_diag(u, B_DK)) for x, u in zip(qk, ub)]
    mc = []
    for o, u, w in zip(ops, ub, wb):
        per_head = []
        for h in range(N_HEADS):
            cols = slice(h * B_DK, (h + 1) * B_DK)
            per_head.append(_dot_tn(o["kd"][h], jnp.concatenate([w[:, cols], u[:, cols]], axis=1)))
        mc.append(per_head)
    return mc, q_eff, o_intra


def _gdn_kernel(qkv_ref, ab_ref, alog_ref, dtb_ref, of_ref, ob_ref, s_ref, m_ref, c_ref, qe_ref, egl_ref, *, group):
    seq = qkv_ref.shape[0]
    nc = seq // CHUNK
    alog = alog_ref[...]
    dtb = dtb_ref[...]
    o_refs = (of_ref, ob_ref)

    def solve(i, carry):
        ns = [i * group + j for j in range(group)]
        rows = [pl.ds(pl.multiple_of(n * CHUNK, CHUNK), CHUNK) for n in ns]
        for d, rev in enumerate((False, True)):
            ops = [_gdn_chunk_operands(qkv_ref, ab_ref, alog, dtb, r, rev) for r in rows]
            mc, q_eff, o_intra = _gdn_solve_chunks(ops)
            for j, n in enumerate(ns):
                slot = (d * nc + n) * N_HEADS
                for h in range(N_HEADS):
                    m_ref[slot + h] = mc[j][h][:, 0:B_DK].astype(BF16)
                    c_ref[slot + h] = mc[j][h][:, B_DK:2 * B_DK].astype(BF16)
                qe_ref[d, rows[j], :] = q_eff[j].astype(BF16)
                o_refs[d][rows[j], :] = o_intra[j].astype(BF16)
                egl_ref[d * nc + n] = jnp.broadcast_to(ops[j]["eg_last"], (8, 128))
        return carry

    lax.fori_loop(0, nc // group, solve, 0)

    s_ref[...] = jnp.zeros(s_ref.shape, F32)

    def scan(i, carry):
        work = []
        for d, n in enumerate((i, nc - 1 - i)):
            rows = pl.ds(pl.multiple_of(n * CHUNK, CHUNK), CHUNK)
            egl = egl_ref[d * nc + n]
            slot = (d * nc + n) * N_HEADS
            for h in range(N_HEADS):
                cols = slice(h * B_DK, (h + 1) * B_DK)
                state = s_ref[d * N_HEADS + h]
                lhs = jnp.concatenate([m_ref[slot + h], qe_ref[d, rows, cols]], axis=0)
                work.append((d, h, rows, cols, state, egl[0:1, d * N_HEADS + h:d * N_HEADS + h + 1], slot,
                             _dot(lhs, state.astype(BF16))))
        new = []
        for d, h, rows, cols, state, decay, slot, r in work:
            new.append((state * decay - r[0:B_DK] + c_ref[slot + h].astype(F32),
                        o_refs[d][rows, cols].astype(F32) + r[B_DK:B_DK + CHUNK]))
        for (d, h, rows, cols, *_), (st, o) in zip(work, new):
            s_ref[d * N_HEADS + h] = st
            o_refs[d][rows, cols] = o.astype(BF16)
        return carry

    lax.fori_loop(0, nc, scan, 0)


def _gdn(qkv, ab, alog, dtb, batch, seq, group=8):
    nc = seq // CHUNK
    out = pl.BlockSpec((seq, W_GDN), lambda b: (b, 0))
    return pl.pallas_call(
        functools.partial(_gdn_kernel, group=group),
        grid=(batch,),
        in_specs=[pl.BlockSpec((seq, 3 * W_GDN), lambda b: (b, 0)),
                  pl.BlockSpec((seq, 128), lambda b: (b, 0)),
                  pl.BlockSpec((1, 128), lambda b: (0, 0)),
                  pl.BlockSpec((1, 128), lambda b: (0, 0))],
        out_specs=[out, out],
        out_shape=[jax.ShapeDtypeStruct((batch * seq, W_GDN), BF16)] * 2,
        scratch_shapes=[pltpu.VMEM((2 * N_HEADS, B_DK, B_DK), F32),
                        pltpu.VMEM((2 * nc * N_HEADS, B_DK, B_DK), BF16),
                        pltpu.VMEM((2 * nc * N_HEADS, B_DK, B_DK), BF16),
                        pltpu.VMEM((2, seq, W_GDN), BF16),
                        pltpu.VMEM((2 * nc, 8, 128), F32)],
        compiler_params=_cparams(("parallel",)),
        name="gdn",
    )(qkv, ab, alog, dtb)


def _ffn_kernel(x_ref, oa_ref, of_ref, ob_ref, z_ref, oc_ref, onorm_ref, wa_ref, wb_ref, wc_ref,
                nw_ref, wg_ref, wu_ref, wd_ref, o_ref, xn_ref, acc_ref):
    j = pl.program_id(1)

    @pl.when(j == 0)
    def _():
        x = x_ref[...] + _dot(_join_halves([oa_ref[0], oa_ref[1]]).astype(BF16), wa_ref[...])
        x = x + _dot(oc_ref[...], wc_ref[...])
        for h in range(N_HEADS):
            cols = slice(h * B_DK, (h + 1) * B_DK)
            o = of_ref[:, cols].astype(F32) + ob_ref[:, cols].astype(F32)
            ms = jnp.mean(o * o, axis=-1, keepdims=True)
            y = (o * lax.rsqrt(ms + EPS)) * onorm_ref[...]
            gated = (y * _silu(z_ref[:, cols].astype(F32))).astype(BF16)
            x = x + _dot(gated, wb_ref[h * B_DK:(h + 1) * B_DK, :])
        ms = jnp.mean(x * x, axis=-1, keepdims=True)
        xn_ref[...] = ((x * lax.rsqrt(ms + EPS)) * nw_ref[...]).astype(BF16)
        acc_ref[...] = x

    hr = xn_ref.shape[0] // 2
    rsl = [slice(h * hr, (h + 1) * hr) for h in range(2)]
    xns = [xn_ref[r, :] for r in rsl]
    gates = [_dot(xn, wg_ref[...]) for xn in xns]
    ups = [_dot(xn, wu_ref[...]) for xn in xns]
    hiddens = [(_silu(g) * u).astype(BF16) for g, u in zip(gates, ups)]
    downs = [_dot(hd, wd_ref[...]) for hd in hiddens]
    for r, dn in zip(rsl, downs):
        acc_ref[r, :] += dn

    @pl.when(j == pl.num_programs(1) - 1)
    def _():
        o_ref[...] = acc_ref[...]


def _out_proj_ffn(x, oa, o_f, o_b, proj_b, oc, onorm, wa, wb, wc, nw, w_gate_up, w_down, tm=512, tf=2816):
    t, d = x.shape
    dff = w_down.shape[0]
    nf = dff // tf
    row = lambda w: pl.BlockSpec((tm, w), lambda i, j: (i, 0))
    full = lambda w: pl.BlockSpec((w, d), lambda i, j: (0, 0))
    return pl.pallas_call(
        _ffn_kernel,
        grid=(t // tm, nf),
        in_specs=[row(d), pl.BlockSpec((2, tm, 128), lambda i, j: (0, i, 0)), row(W_GDN), row(W_GDN),
                  pl.BlockSpec((tm, W_GDN), lambda i, j: (i, COL_BZ // W_GDN)),
                  row(W_ATT), pl.BlockSpec((1, B_DK), lambda i, j: (0, 0)),
                  full(W_ATT), full(W_GDN), full(W_ATT),
                  pl.BlockSpec((1, d), lambda i, j: (0, 0)),
                  pl.BlockSpec((d, tf), lambda i, j: (0, j), pipeline_mode=pl.Buffered(1)),
                  pl.BlockSpec((d, tf), lambda i, j: (0, j + nf), pipeline_mode=pl.Buffered(1)),
                  pl.BlockSpec((tf, d), lambda i, j: (j, 0), pipeline_mode=pl.Buffered(1))],
        out_specs=pl.BlockSpec((tm, d), lambda i, j: (i, 0)),
        out_shape=jax.ShapeDtypeStruct((t, d), F32),
        scratch_shapes=[pltpu.VMEM((tm, d), BF16), pltpu.VMEM((tm, d), F32)],
        compiler_params=_cparams(("parallel", "arbitrary")),
        name="ffn",
    )(x, oa, o_f, o_b, proj_b, oc, onorm, wa, wb, wc, nw, w_gate_up, w_gate_up, w_down)


def _rope_tables(seq):
    t = np.arange(seq, dtype=np.float64)

    def build(groups):
        cos = np.ones((seq, HEAD_DIM))
        s_lo = np.zeros((seq, HEAD_DIM))
        s_hi = np.zeros((seq, HEAD_DIM))
        for start, half, theta, pos in groups:
            inv = np.float64(np.float32(theta)) ** (-np.arange(half, dtype=np.float64) / half)
            ang = pos[:, None] * inv[None, :]
            cos[:, start:start + half] = np.cos(ang)
            cos[:, start + half:start + 2 * half] = np.cos(ang)
            s_lo[:, start:start + half] = -np.sin(ang)
            s_hi[:, start + half:start + 2 * half] = np.sin(ang)
        tab = np.stack([cos, s_lo, s_hi])
        return np.tile(tab, (1, 1, N_HEADS)).astype(np.float32)

    tab_a = build([(0, A_ROT_HALF, A_THETA, t)])
    row = np.floor(t / GRID_W)
    col = t - row * GRID_W
    tab_c = build([(0, C_ROT_HALF, C_THETA, row), (2 * C_ROT_HALF, C_ROT_HALF, C_THETA, col)])
    return tab_a, tab_c


def _multiplicity_tables(seq):
    (w0, d0), rest = A_PATTERNS[0], A_PATTERNS[1:]
    assert d0 == 1 and all(d % N_CLS == 0 for _, d in rest)
    r0 = w0 // 2
    tq, win = A_LOCAL_TQ, A_LOCAL_TQ + 2 * r0
    qi = np.arange(tq)[:, None]
    kj = np.arange(win)[None, :]
    local = np.stack([(np.abs(qi + off - kj) <= r0) for off in (0, r0, 2 * r0)]).astype(np.float32)
    n = seq // N_CLS
    d = np.arange(n)[:, None] - np.arange(n)[None, :]
    dilated = np.zeros((n, n), np.float32)
    for window, dil in rest:
        step = dil // N_CLS
        radius = window // (2 * dil)
        dilated += ((d % step == 0) & (np.abs(d) <= radius * step)).astype(np.float32)
    return local, dilated


def _head_mean_matrix():
    e = np.kron(np.eye(N_HEADS), np.full((HEAD_DIM, HEAD_DIM), 1.0 / HEAD_DIM))
    return e.astype(np.float32)


def _pad_lanes(v, width):
    return jnp.pad(v.reshape(1, -1), ((0, 0), (0, width - v.size)))


def kernel(x, norm1, w_in, qn_a, kn_a, conv_b, a_log_b, dt_bias_b, onorm_b, qn_c, kn_c, w_out, norm2, w_gate_up, w_down):
    batch, seq, d = x.shape
    depth = w_in.shape[0]
    t = batch * seq

    tab_a, tab_c = _rope_tables(seq)
    tab_a, tab_c = jnp.asarray(tab_a), jnp.asarray(tab_c)
    mult_local, mult_dil = _multiplicity_tables(seq)
    mult_local, mult_dil = jnp.asarray(mult_local, dtype=BF16), jnp.asarray(mult_dil, dtype=BF16)
    e_mat = jnp.asarray(_head_mean_matrix(), dtype=BF16)

    c0 = 3 * W_ATT
    c1 = c0 + 4 * W_GDN
    c2 = c1 + 4 * N_HEADS
    w_in_h = w_in.astype(BF16)
    cq = [w_in_h[:, :, c2 + h * HEAD_DIM:c2 + (h + 1) * HEAD_DIM] for h in C_HEAD_ORDER]
    w_in_p = jnp.concatenate(
        [w_in_h[:, :, c0:c1], w_in_h[:, :, 0:c0]] + cq + [w_in_h[:, :, c2 + W_ATT:], w_in_h[:, :, c1:c2],
         jnp.zeros(w_in.shape[:2] + (IN_PAD - w_in.shape[2],), BF16)], axis=-1)

    w_out_h = w_out.astype(BF16)
    w_out_a = w_out_h[:, 0:W_ATT]
    w_out_b = w_out_h[:, W_ATT:W_ATT + W_GDN]
    wc = w_out_h[:, W_ATT + W_GDN:]
    w_out_c = jnp.concatenate([wc[:, h * HEAD_DIM:(h + 1) * HEAD_DIM] for h in C_HEAD_ORDER], axis=1)
    w_gu = w_gate_up.astype(BF16)
    w_dn = w_down.astype(BF16)

    tile4 = lambda v: jnp.tile(v.reshape(1, -1), (1, N_HEADS))

    xt = x.reshape(t, d)
    for i in range(depth):
        proj, proj_ab, (qa_cls, ka_cls, va_cls), (qa, ka, va, qc, kc, vc) = _in_proj(
            xt, norm1[i].reshape(1, d), w_in_p[i], tile4(qn_a[i]), tile4(kn_a[i]), tile4(qn_c[i]), tile4(kn_c[i]),
            tab_a, tab_c, e_mat, batch, seq)
        o1, lse1 = _attn_a_local(qa, ka, va, mult_local, batch, seq)
        o_a = _attn_a_dilated(qa_cls, ka_cls, va_cls, mult_dil, o1, lse1, batch, seq)
        o_c = _attn_c(qc, kc, vc, batch, seq)
        qkv_b = _gdn_prep(proj, conv_b[i], batch, seq)
        alog, dtb = _pad_lanes(a_log_b[i], 128), _pad_lanes(dt_bias_b[i], 128)
        o_f, o_r = _gdn(qkv_b, proj_ab, alog, dtb, batch, seq)
        xt = _out_proj_ffn(xt, o_a, o_f, o_r, proj, o_c, onorm_b[i].reshape(1, B_DK),
                           w_out_a[i], w_out_b[i], w_out_c[i], norm2[i].reshape(1, d), w_gu[i], w_dn[i])
    return xt.reshape(batch, seq, d)
```

```python
import functools

import numpy as np
import jax
import jax.numpy as jnp
from jax import lax
from jax.experimental import pallas as pl
from jax.experimental.pallas import tpu as pltpu

F32 = jnp.float32
BF16 = jnp.bfloat16

EPS = 1e-6
GRID_W = 64
HEAD_DIM = 64
N_HEADS = 4
A_PATTERNS = ((128, 1), (512, 4), (2048, 16))
A_ROT_HALF = 8
A_THETA = 500000.0
C_ROT_HALF = 16
C_THETA = 10000.0
B_DK = 128
B_CONV = 5
CHUNK = 64
NEG = -1e30

W_ATT = N_HEADS * HEAD_DIM
W_GDN = N_HEADS * B_DK
COL_BQ, COL_BK, COL_BV, COL_BZ = 0, 512, 1024, 1536
COL_AQ, COL_AK, COL_AV = 2048, 2304, 2560
COL_CQ, COL_CK, COL_CV = 2816, 3072, 3200
COL_AB = 3328
IN_PAD = 3456
N_CLS = 4
A_LOCAL_TQ = 128
C_HEAD_ORDER = (0, 2, 1, 3)

VMEM_LIMIT = 56 * 1024 * 1024


def _cparams(sem):
    return pltpu.CompilerParams(dimension_semantics=sem, vmem_limit_bytes=VMEM_LIMIT)


def _split2(x):
    hi = x.astype(BF16)
    lo = (x - hi.astype(F32)).astype(BF16)
    return hi, lo


def _dot(a, b):
    return jnp.dot(a, b, preferred_element_type=F32)


def _dot_nt(a, b):
    return lax.dot_general(a, b, (((1,), (1,)), ((), ())), preferred_element_type=F32)


def _dot_tn(a, b):
    return lax.dot_general(a, b, (((0,), (0,)), ((), ())), preferred_element_type=F32)


def _sigmoid(x):
    return 1.0 / (1.0 + jnp.exp(-x))


def _silu(x):
    hx = 0.5 * x
    return hx + hx * jnp.tanh(hx)


def _head_norm_rope(x, w, e_hi, cos, sin_lo, sin_hi, rot_half, scale):
    xx = x * x
    hi, lo = _split2(xx)
    ms = _dot(hi, e_hi) + _dot(lo, e_hi)
    y = (x * lax.rsqrt(ms + EPS)) * w
    width = x.shape[1]
    up = pltpu.roll(y, width - rot_half, 1)
    dn = pltpu.roll(y, rot_half, 1)
    out = y * cos + up * sin_lo + dn * sin_hi
    return out * scale


def _join_halves(halves):
    return jnp.concatenate(halves, axis=1)


def _inproj_kernel(x_ref, nw_ref, w_ref, qna_ref, kna_ref, qnc_ref, knc_ref, ta_ref, tc_ref, e_ref,
                   ob_ref, oab_ref, oqa_cls_ref, oka_cls_ref, ova_cls_ref, oqa_ref, oka_ref, ova_ref,
                   oqc_ref, okc_ref, ovc_ref, scr_ref):
    tm = x_ref.shape[0]
    nh = 4
    hr = tm // nh
    rsl = [slice(h * hr, (h + 1) * hr) for h in range(nh)]
    ys = []
    for r in rsl:
        x = x_ref[r, :]
        ms = jnp.mean(x * x, axis=-1, keepdims=True)
        ys.append(((x * lax.rsqrt(ms + EPS)) * nw_ref[...]).astype(BF16))
    ress = [_dot(y, w_ref[...]) for y in ys]

    e = e_ref[...]
    scale = HEAD_DIM ** -0.5
    crows = hr // N_CLS
    for h, (r, res) in enumerate(zip(rsl, ress)):
        ob_ref[r, :] = res[:, COL_BQ:COL_AQ].astype(BF16)
        oab_ref[r, :] = res[:, COL_AB:IN_PAD]
        cos_a, sl_a, sh_a = ta_ref[0, r, :], ta_ref[1, r, :], ta_ref[2, r, :]
        cos_c, sl_c, sh_c = tc_ref[0, r, :], tc_ref[1, r, :], tc_ref[2, r, :]

        def emit(v, tok_ref, cls_ref):
            tok_ref[r, :] = v.astype(BF16)
            for j in range(2):
                scr_ref[h, j] = v[:, j * 128:(j + 1) * 128]
            for c in range(N_CLS):
                grouped = _join_halves([scr_ref[h, j, pl.ds(c, crows, stride=N_CLS), :] for j in range(2)])
                cls_ref[c, h * crows:(h + 1) * crows, :] = grouped.astype(BF16)

        col = lambda c0, w: res[:, c0:c0 + w]
        emit(_head_norm_rope(col(COL_AQ, 256), qna_ref[...], e, cos_a, sl_a, sh_a, A_ROT_HALF, scale), oqa_ref, oqa_cls_ref)
        emit(_head_norm_rope(col(COL_AK, 256), kna_ref[...], e, cos_a, sl_a, sh_a, A_ROT_HALF, 1.0), oka_ref, oka_cls_ref)
        emit(col(COL_AV, 256), ova_ref, ova_cls_ref)
        oqc_ref[r, :] = _head_norm_rope(col(COL_CQ, 256), qnc_ref[...], e, cos_c, sl_c, sh_c, C_ROT_HALF, scale).astype(BF16)
        okc_ref[r, :] = _head_norm_rope(col(COL_CK, 128), knc_ref[...][:, :128], e[:128, :128],
                                        cos_c[:, :128], sl_c[:, :128], sh_c[:, :128], C_ROT_HALF, 1.0).astype(BF16)
        ovc_ref[r, :] = col(COL_CV, 128).astype(BF16)


def _in_proj(x, nw, w, qna, kna, qnc, knc, tab_a, tab_c, e_mat, batch, seq, tm=1024):
    t, d = x.shape
    n = w.shape[1]
    ns = seq // tm
    vec = pl.BlockSpec((1, W_ATT), lambda i: (0, 0))
    tab = pl.BlockSpec((3, tm, W_ATT), lambda i: (0, i % ns, 0))
    ob = lambda wd: pl.BlockSpec((tm, wd), lambda i: (i, 0))
    cls = pl.BlockSpec((None, N_CLS, tm // N_CLS, W_ATT), lambda i: (i // ns, 0, i % ns, 0))
    widths = (256, 256, 256, 256, 128, 128)
    outs = pl.pallas_call(
        _inproj_kernel,
        grid=(t // tm,),
        in_specs=[pl.BlockSpec((tm, d), lambda i: (i, 0)),
                  pl.BlockSpec((1, d), lambda i: (0, 0)),
                  pl.BlockSpec((d, n), lambda i: (0, 0)),
                  vec, vec, vec, vec, tab, tab,
                  pl.BlockSpec((W_ATT, W_ATT), lambda i: (0, 0))],
        out_specs=[ob(COL_AQ), ob(n - COL_AB), cls, cls, cls] + [ob(wd) for wd in widths],
        out_shape=[jax.ShapeDtypeStruct((t, COL_AQ), BF16), jax.ShapeDtypeStruct((t, n - COL_AB), F32)]
                  + [jax.ShapeDtypeStruct((batch, N_CLS, seq // N_CLS, W_ATT), BF16)] * 3
                  + [jax.ShapeDtypeStruct((t, wd), BF16) for wd in widths],
        scratch_shapes=[pltpu.VMEM((4, 2, tm // 4, 128), F32)],
        compiler_params=_cparams(("parallel",)),
        name="in_proj",
    )(x, nw, w, qna, kna, qnc, knc, tab_a, tab_c, e_mat)
    return outs[0], outs[1], outs[2:5], outs[5:]


def _head_masks(width):
    lane = lax.broadcasted_iota(jnp.int32, (1, width), 1)
    return [(lane >= h * HEAD_DIM) & (lane < (h + 1) * HEAD_DIM) for h in range(width // HEAD_DIM)]


def _attn_a_local_kernel(q_ref, k_ref, v_ref, m_ref, o1_ref, lse_ref):
    i = pl.program_id(1)
    seq = k_ref.shape[0]
    tq, win = m_ref.shape[1], m_ref.shape[2]
    nblk = q_ref.shape[0] // tq
    last = seq // tq - 1
    hms = _head_masks(W_ATT)
    vs, scores = [], []
    for j in range(nblk):
        blk = i * nblk + j
        start = jnp.clip(blk * tq - (win - tq) // 2, 0, seq - win)
        rows = pl.ds(pl.multiple_of(start, 64), win)
        k = k_ref[rows, :]
        vs.append(v_ref[rows, :])
        mult = m_ref[1]
        if j == 0:
            mult = jnp.where(blk == 0, m_ref[0], mult)
        if j == nblk - 1:
            mult = jnp.where(blk == last, m_ref[2], mult)
        q = q_ref[j * tq:(j + 1) * tq, :]
        scores.append([jnp.where(mult > 0, _dot_nt(jnp.where(hm, q, jnp.zeros_like(q)), k), NEG) for hm in hms])
    maxes = [[jnp.max(s, axis=-1, keepdims=True) for s in row] for row in scores]
    probs = [[jnp.exp(s - m) for s, m in zip(srow, mrow)] for srow, mrow in zip(scores, maxes)]
    sums = [[jnp.sum(p, axis=-1, keepdims=True) for p in row] for row in probs]
    pvs = [[_dot(p.astype(BF16), v) for p in row] for row, v in zip(probs, vs)]
    for j in range(nblk):
        out = jnp.zeros((tq, W_ATT), F32)
        lse = jnp.zeros((tq, W_ATT), F32)
        for h in range(N_HEADS):
            out = jnp.where(hms[h], pvs[j][h] * (1.0 / sums[j][h]), out)
            lse = jnp.where(hms[h], maxes[j][h] + jnp.log(sums[j][h]), lse)
        for half in range(2):
            o1_ref[half, j * tq:(j + 1) * tq, :] = out[:, half * 128:(half + 1) * 128]
            lse_ref[half, j * tq:(j + 1) * tq, :] = lse[:, half * 128:(half + 1) * 128]


def _attn_a_local(q, k, v, mult, batch, seq, nblk=16):
    tq = mult.shape[1]
    rows = nblk * tq
    nq = seq // rows
    return pl.pallas_call(
        _attn_a_local_kernel,
        grid=(batch, nq),
        in_specs=[pl.BlockSpec((rows, W_ATT), lambda b, i: (b * nq + i, 0)),
                  pl.BlockSpec((seq, W_ATT), lambda b, i: (b, 0)),
                  pl.BlockSpec((seq, W_ATT), lambda b, i: (b, 0)),
                  pl.BlockSpec(mult.shape, lambda b, i: (0, 0, 0))],
        out_specs=[pl.BlockSpec((2, rows, 128), lambda b, i: (0, b * nq + i, 0)),
                   pl.BlockSpec((2, rows, 128), lambda b, i: (0, b * nq + i, 0))],
        out_shape=[jax.ShapeDtypeStruct((2, batch * seq, 128), F32), jax.ShapeDtypeStruct((2, batch * seq, 128), F32)],
        compiler_params=_cparams(("parallel", "parallel")),
        name="attn_a_local",
    )(q, k, v, mult)


def _attn_a_dilated_kernel(q_ref, k_ref, v_ref, m_ref, o1_ref, lse1_ref, o_ref):
    tq = q_ref.shape[1]
    mult = m_ref[...]
    multf = mult.astype(F32)
    hms = _head_masks(W_ATT)
    qs = [q_ref[c] for c in range(N_CLS)]
    scores = [[jnp.where(mult > 0, _dot_nt(jnp.where(hm, q, jnp.zeros_like(q)), k_ref[c]), NEG) for hm in hms]
              for c, q in enumerate(qs)]
    m2 = [[jnp.max(s, axis=-1, keepdims=True) for s in row] for row in scores]
    probs = [[jnp.exp(s - m) * multf for s, m in zip(srow, mrow)] for srow, mrow in zip(scores, m2)]
    l2 = [[jnp.sum(p, axis=-1, keepdims=True) for p in row] for row in probs]
    pv2 = [[_dot(p.astype(BF16), v_ref[c]) for p in row] for c, row in enumerate(probs)]
    for c in range(N_CLS):
        m2b = jnp.zeros((tq, W_ATT), F32)
        l2b = jnp.zeros((tq, W_ATT), F32)
        pv2b = jnp.zeros((tq, W_ATT), F32)
        for h in range(N_HEADS):
            m2b = jnp.where(hms[h], m2[c][h], m2b)
            l2b = jnp.where(hms[h], l2[c][h], l2b)
            pv2b = jnp.where(hms[h], pv2[c][h], pv2b)
        tok = pl.ds(c, tq, stride=N_CLS)
        lse1 = _join_halves([lse1_ref[half, tok, :] for half in range(2)])
        o1 = _join_halves([o1_ref[half, tok, :] for half in range(2)])
        m = jnp.maximum(lse1, m2b)
        a1 = jnp.exp(lse1 - m)
        a2 = jnp.exp(m2b - m)
        out = (o1 * a1 + pv2b * a2) / (a1 + l2b * a2)
        for half in range(2):
            o_ref[half, tok, :] = out[:, half * 128:(half + 1) * 128]


def _attn_a_dilated(q, k, v, mult, o1, lse1, batch, seq, tq=512):
    rows = seq // N_CLS
    nu = rows // tq
    tok = pl.BlockSpec((2, N_CLS * tq, 128), lambda u, b: (0, b * nu + u, 0))
    return pl.pallas_call(
        _attn_a_dilated_kernel,
        grid=(nu, batch),
        in_specs=[pl.BlockSpec((None, N_CLS, tq, W_ATT), lambda u, b: (b, 0, u, 0)),
                  pl.BlockSpec((None, N_CLS, rows, W_ATT), lambda u, b: (b, 0, 0, 0)),
                  pl.BlockSpec((None, N_CLS, rows, W_ATT), lambda u, b: (b, 0, 0, 0)),
                  pl.BlockSpec((tq, rows), lambda u, b: (u, 0)),
                  tok, tok],
        out_specs=tok,
        out_shape=jax.ShapeDtypeStruct((2, batch * seq, 128), F32),
        compiler_params=_cparams(("parallel", "parallel")),
        name="attn_a_dilated",
    )(q, k, v, mult, o1, lse1)


def _attn_c_kernel(q_ref, k_ref, v_ref, o_ref):
    k = k_ref[...]
    v = v_ref[...]
    hms = _head_masks(128)
    qs = [q_ref[:, pair * 128:(pair + 1) * 128] for pair in range(2)]
    scores = [[_dot_nt(jnp.where(hm, q, jnp.zeros_like(q)), k) for hm in hms] for q in qs]
    maxes = [[jnp.max(s, axis=-1, keepdims=True) for s in row] for row in scores]
    probs = [[jnp.exp(s - m) for s, m in zip(srow, mrow)] for srow, mrow in zip(scores, maxes)]
    sums = [[jnp.sum(p, axis=-1, keepdims=True) for p in row] for row in probs]
    pvs = [[_dot(p.astype(BF16), v) for p in row] for row in probs]
    for pair in range(2):
        acc = jnp.zeros(qs[pair].shape, F32)
        for half in range(2):
            acc = jnp.where(hms[half], pvs[pair][half] * (1.0 / sums[pair][half]), acc)
        o_ref[:, pair * 128:(pair + 1) * 128] = acc.astype(o_ref.dtype)


def _attn_c(q, k, v, batch, seq, tq=512):
    nq = seq // tq
    return pl.pallas_call(
        _attn_c_kernel,
        grid=(batch, nq),
        in_specs=[pl.BlockSpec((tq, W_ATT), lambda b, i: (b * nq + i, 0)),
                  pl.BlockSpec((seq, 128), lambda b, i: (b, 0)),
                  pl.BlockSpec((seq, 128), lambda b, i: (b, 0))],
        out_specs=pl.BlockSpec((tq, W_ATT), lambda b, i: (b * nq + i, 0)),
        out_shape=jax.ShapeDtypeStruct((batch * seq, W_ATT), BF16),
        compiler_params=_cparams(("parallel", "parallel")),
        name="attn_c",
    )(q, k, v)


def _gdn_prep_kernel(x_ref, w_ref, o_ref):
    j = pl.program_id(1)
    x = x_ref[...].astype(F32)
    w = w_ref[...]
    seq = x.shape[0]
    pos = lax.broadcasted_iota(jnp.int32, (seq, 1), 0)
    acc = x * w[B_CONV // 2:B_CONV // 2 + 1, :]
    for tap in range(B_CONV):
        off = tap - B_CONV // 2
        if off == 0:
            continue
        sh = pltpu.roll(x, (-off) % seq, 0)
        ok = (pos + off >= 0) & (pos + off < seq)
        acc = acc + jnp.where(ok, sh, 0.0) * w[tap:tap + 1, :]
    y = _silu(acc)
    norm_scale = jnp.where(j == 0, B_DK ** -0.5, 1.0).astype(F32)
    for h in range(N_HEADS):
        yh = y[:, h * B_DK:(h + 1) * B_DK]
        ss = jnp.sum(yh * yh, axis=-1, keepdims=True)
        nh = yh * (lax.rsqrt(ss + EPS) * norm_scale)
        o_ref[:, h * B_DK:(h + 1) * B_DK] = jnp.where(j < 2, nh, yh).astype(o_ref.dtype)


def _gdn_prep(proj, conv_w, batch, seq):
    return pl.pallas_call(
        _gdn_prep_kernel,
        grid=(batch, 3),
        in_specs=[pl.BlockSpec((seq, W_GDN), lambda b, j: (b, j)),
                  pl.BlockSpec((B_CONV, W_GDN), lambda b, j: (0, j))],
        out_specs=pl.BlockSpec((seq, W_GDN), lambda b, j: (b, j)),
        out_shape=jax.ShapeDtypeStruct((batch * seq, 3 * W_GDN), BF16),
        compiler_params=_cparams(("parallel", "parallel")),
        name="gdn_prep",
    )(proj, conv_w)


def _lane_block(width, block, h):
    lane = lax.broadcasted_iota(jnp.int32, (1, width), 1)
    return (lane >= h * block) & (lane < (h + 1) * block)


def _block_diag(x, block):
    width = x.shape[1]
    zero = jnp.zeros_like(x)
    return jnp.concatenate([jnp.where(_lane_block(width, block, h), x, zero) for h in range(N_HEADS)], axis=0)


def _unit_tri_inverse_cat(a_list, eye_t):
    p = [eye_t - a for a in a_list]
    ab = [a.astype(BF16) for a in a_list]
    ap = [_dot(x, _block_diag(x, CHUNK)) for x in ab]
    for lvl in range(5):
        apb = [x.astype(BF16) for x in ap]
        bd = [_block_diag(x, CHUNK) for x in apb]
        if lvl < 4:
            r = [_dot(jnp.concatenate([pi.astype(BF16), xi], axis=0), bi) for pi, xi, bi in zip(p, apb, bd)]
            p = [pi + ri[0:CHUNK] for pi, ri in zip(p, r)]
            ap = [ri[CHUNK:2 * CHUNK] for ri in r]
        else:
            p = [pi + _dot(pi.astype(BF16), bi) for pi, bi in zip(p, bd)]
    return p


def _chunk_cumsum(g, rev):
    ridx = lax.broadcasted_iota(jnp.int32, g.shape, 0)
    x = g
    s = 1
    while s < CHUNK:
        if rev:
            sh = jnp.where(ridx < CHUNK - s, pltpu.roll(x, CHUNK - s, 0), 0.0)
        else:
            sh = jnp.where(ridx >= s, pltpu.roll(x, s, 0), 0.0)
        x = x + sh
        s *= 2
    return x


def _gdn_chunk_operands(qkv_ref, ab_ref, alog, dtb, rows, rev):
    c0 = N_HEADS if rev else 0
    ab = ab_ref[rows, :]
    xg = ab + dtb
    softplus = jnp.maximum(xg, 0.0) + jnp.log(1.0 + jnp.exp(-jnp.abs(xg)))
    g = -jnp.exp(alog) * softplus
    beta = _sigmoid(ab)
    gc = _chunk_cumsum(g, rev)
    g_last = gc[0:1, :] if rev else gc[CHUNK - 1:CHUNK, :]
    eg = jnp.exp(gc)
    e_rest = jnp.exp(g_last - gc)

    wcat = N_HEADS * CHUNK
    ii = lax.broadcasted_iota(jnp.int32, (CHUNK, 1), 0)
    jl = lax.broadcasted_iota(jnp.int32, (1, wcat), 1) & (CHUNK - 1)
    eye_b = ii == jl
    incl = (ii <= jl) if rev else (ii >= jl)
    g_col = jnp.zeros((CHUNK, wcat), F32)
    for h in range(N_HEADS):
        g_col = jnp.where(_lane_block(wcat, CHUNK, h), gc[:, c0 + h:c0 + h + 1], g_col)
    g_row = jnp.sum(jnp.where(eye_b, g_col, 0.0), axis=0, keepdims=True)
    dec = jnp.exp(jnp.where(incl, g_col - g_row, NEG))

    kb, vb, kbg, qg, kd = [], [], [], [], []
    for h in range(N_HEADS):
        c = c0 + h
        qh = qkv_ref[rows, h * B_DK:(h + 1) * B_DK].astype(F32)
        kh = qkv_ref[rows, W_GDN + h * B_DK:W_GDN + (h + 1) * B_DK].astype(F32)
        vh = qkv_ref[rows, 2 * W_GDN + h * B_DK:2 * W_GDN + (h + 1) * B_DK].astype(F32)
        bcol = beta[:, 8 + c:9 + c]
        egc = eg[:, c:c + 1]
        kbh = kh * bcol
        kb.append(kbh)
        vb.append(vh * bcol)
        kbg.append(kbh * egc)
        qg.append(qh * egc)
        kd.append((kh * e_rest[:, c:c + 1]).astype(BF16))
    cat = lambda xs: jnp.concatenate(xs, axis=1)
    return dict(dec=dec, eye_b=eye_b, kb=cat(kb).astype(BF16), vb=cat(vb).astype(BF16), kbg=cat(kbg).astype(BF16),
                qg=cat(qg), kd=kd, eg_last=jnp.exp(g_last),
                k=qkv_ref[rows, W_GDN:2 * W_GDN], q=qkv_ref[rows, 0:W_GDN])


def _gdn_solve_chunks(ops):
    kq = [_dot_nt(jnp.concatenate([o["kb"], o["q"]], axis=0), _block_diag(o["k"], B_DK)) for o in ops]
    a_cat = [jnp.where(o["eye_b"], 0.0, x[0:CHUNK] * o["dec"]) for o, x in zip(ops, kq)]
    qk = [(x[CHUNK:2 * CHUNK] * o["dec"]).astype(BF16) for o, x in zip(ops, kq)]
    eye_t = jnp.where(ops[0]["eye_b"], 1.0, 0.0).astype(F32)
    t_cat = [t.astype(BF16) for t in _unit_tri_inverse_cat(a_cat, eye_t)]
    ub = [_dot(t, _block_diag(o["vb"], B_DK)).astype(BF16) for o, t in zip(ops, t_cat)]
    wb = [_dot(t, _block_diag(o["kbg"], B_DK)).astype(BF16) for o, t in zip(ops, t_cat)]
    q_eff = [o["qg"] - _dot(x, _block_diag(w, B_DK)) for o, x, w in zip(ops, qk, wb)]
    o_intra = [_dot(x, _block_diag(u, B_DK)) for x, u in zip(qk, ub)]
    mc = []
    for o, u, w in zip(ops, ub, wb):
        per_head = []
        for h in range(N_HEADS):
            cols = slice(h * B_DK, (h + 1) * B_DK)
            per_head.append(_dot_tn(o["kd"][h], jnp.concatenate([w[:, cols], u[:, cols]], axis=1)))
        mc.append(per_head)
    return mc, q_eff, o_intra


def _gdn_kernel(qkv_ref, ab_ref, alog_ref, dtb_ref, of_ref, ob_ref, s_ref, m_ref, c_ref, qe_ref, egl_ref, *, group):
    seq = qkv_ref.shape[0]
    nc = seq // CHUNK
    alog = alog_ref[...]
    dtb = dtb_ref[...]
    o_refs = (of_ref, ob_ref)

    def solve(i, carry):
        ns = [i * group + j for j in range(group)]
        rows = [pl.ds(pl.multiple_of(n * CHUNK, CHUNK), CHUNK) for n in ns]
        for d, rev in enumerate((False, True)):
            ops = [_gdn_chunk_operands(qkv_ref, ab_ref, alog, dtb, r, rev) for r in rows]
            mc, q_eff, o_intra = _gdn_solve_chunks(ops)
            for j, n in enumerate(ns):
                slot = (d * nc + n) * N_HEADS
                for h in range(N_HEADS):
                    m_ref[slot + h] = mc[j][h][:, 0:B_DK].astype(BF16)
                    c_ref[slot + h] = mc[j][h][:, B_DK:2 * B_DK].astype(BF16)
                qe_ref[d, rows[j], :] = q_eff[j].astype(BF16)
                o_refs[d][rows[j], :] = o_intra[j].astype(BF16)
                egl_ref[d * nc + n] = jnp.broadcast_to(ops[j]["eg_last"], (8, 128))
        return carry

    lax.fori_loop(0, nc // group, solve, 0)

    s_ref[...] = jnp.zeros(s_ref.shape, F32)

    def scan(i, carry):
        work = []
        for d, n in enumerate((i, nc - 1 - i)):
            rows = pl.ds(pl.multiple_of(n * CHUNK, CHUNK), CHUNK)
            egl = egl_ref[d * nc + n]
            slot = (d * nc + n) * N_HEADS
            for h in range(N_HEADS):
                cols = slice(h * B_DK, (h + 1) * B_DK)
                state = s_ref[d * N_HEADS + h]
                lhs = jnp.concatenate([m_ref[slot + h], qe_ref[d, rows, cols]], axis=0)
                work.append((d, h, rows, cols, state, egl[0:1, d * N_HEADS + h:d * N_HEADS + h + 1], slot,
                             _dot(lhs, state.astype(BF16))))
        new = []
        for d, h, rows, cols, state, decay, slot, r in work:
            new.append((state * decay - r[0:B_DK] + c_ref[slot + h].astype(F32),
                        o_refs[d][rows, cols].astype(F32) + r[B_DK:B_DK + CHUNK]))
        for (d, h, rows, cols, *_), (st, o) in zip(work, new):
            s_ref[d * N_HEADS + h] = st
            o_refs[d][rows, cols] = o.astype(BF16)
        return carry

    lax.fori_loop(0, nc, scan, 0)


def _gdn(qkv, ab, alog, dtb, batch, seq, group=8):
    nc = seq // CHUNK
    out = pl.BlockSpec((seq, W_GDN), lambda b: (b, 0))
    return pl.pallas_call(
        functools.partial(_gdn_kernel, group=group),
        grid=(batch,),
        in_specs=[pl.BlockSpec((seq, 3 * W_GDN), lambda b: (b, 0)),
                  pl.BlockSpec((seq, 128), lambda b: (b, 0)),
                  pl.BlockSpec((1, 128), lambda b: (0, 0)),
                  pl.BlockSpec((1, 128), lambda b: (0, 0))],
        out_specs=[out, out],
        out_shape=[jax.ShapeDtypeStruct((batch * seq, W_GDN), BF16)] * 2,
        scratch_shapes=[pltpu.VMEM((2 * N_HEADS, B_DK, B_DK), F32),
                        pltpu.VMEM((2 * nc * N_HEADS, B_DK, B_DK), BF16),
                        pltpu.VMEM((2 * nc * N_HEADS, B_DK, B_DK), BF16),
                        pltpu.VMEM((2, seq, W_GDN), BF16),
                        pltpu.VMEM((2 * nc, 8, 128), F32)],
        compiler_params=_cparams(("parallel",)),
        name="gdn",
    )(qkv, ab, alog, dtb)


def _ffn_kernel(x_ref, oa_ref, of_ref, ob_ref, z_ref, oc_ref, onorm_ref, wa_ref, wb_ref, wc_ref,
                nw_ref, wg_ref, wu_ref, wd_ref, o_ref, xn_ref, acc_ref):
    j = pl.program_id(1)

    @pl.when(j == 0)
    def _():
        x = x_ref[...] + _dot(_join_halves([oa_ref[0], oa_ref[1]]).astype(BF16), wa_ref[...])
        x = x + _dot(oc_ref[...], wc_ref[...])
        for h in range(N_HEADS):
            cols = slice(h * B_DK, (h + 1) * B_DK)
            o = of_ref[:, cols].astype(F32) + ob_ref[:, cols].astype(F32)
            ms = jnp.mean(o * o, axis=-1, keepdims=True)
            y = (o * lax.rsqrt(ms + EPS)) * onorm_ref[...]
            gated = (y * _silu(z_ref[:, cols].astype(F32))).astype(BF16)
            x = x + _dot(gated, wb_ref[h * B_DK:(h + 1) * B_DK, :])
        ms = jnp.mean(x * x, axis=-1, keepdims=True)
        xn_ref[...] = ((x * lax.rsqrt(ms + EPS)) * nw_ref[...]).astype(BF16)
        acc_ref[...] = x

    hr = xn_ref.shape[0] // 2
    rsl = [slice(h * hr, (h + 1) * hr) for h in range(2)]
    xns = [xn_ref[r, :] for r in rsl]
    gates = [_dot(xn, wg_ref[...]) for xn in xns]
    ups = [_dot(xn, wu_ref[...]) for xn in xns]
    hiddens = [(_silu(g) * u).astype(BF16) for g, u in zip(gates, ups)]
    downs = [_dot(hd, wd_ref[...]) for hd in hiddens]
    for r, dn in zip(rsl, downs):
        acc_ref[r, :] += dn

    @pl.when(j == pl.num_programs(1) - 1)
    def _():
        o_ref[...] = acc_ref[...]


def _out_proj_ffn(x, oa, o_f, o_b, proj_b, oc, onorm, wa, wb, wc, nw, w_gate_up, w_down, tm=512, tf=2816):
    t, d = x.shape
    dff = w_down.shape[0]
    nf = dff // tf
    row = lambda w: pl.BlockSpec((tm, w), lambda i, j: (i, 0))
    full = lambda w: pl.BlockSpec((w, d), lambda i, j: (0, 0))
    return pl.pallas_call(
        _ffn_kernel,
        grid=(t // tm, nf),
        in_specs=[row(d), pl.BlockSpec((2, tm, 128), lambda i, j: (0, i, 0)), row(W_GDN), row(W_GDN),
                  pl.BlockSpec((tm, W_GDN), lambda i, j: (i, COL_BZ // W_GDN)),
                  row(W_ATT), pl.BlockSpec((1, B_DK), lambda i, j: (0, 0)),
                  full(W_ATT), full(W_GDN), full(W_ATT),
                  pl.BlockSpec((1, d), lambda i, j: (0, 0)),
                  pl.BlockSpec((d, tf), lambda i, j: (0, j), pipeline_mode=pl.Buffered(1)),
                  pl.BlockSpec((d, tf), lambda i, j: (0, j + nf), pipeline_mode=pl.Buffered(1)),
                  pl.BlockSpec((tf, d), lambda i, j: (j, 0), pipeline_mode=pl.Buffered(1))],
        out_specs=pl.BlockSpec((tm, d), lambda i, j: (i, 0)),
        out_shape=jax.ShapeDtypeStruct((t, d), F32),
        scratch_shapes=[pltpu.VMEM((tm, d), BF16), pltpu.VMEM((tm, d), F32)],
        compiler_params=_cparams(("parallel", "arbitrary")),
        name="ffn",
    )(x, oa, o_f, o_b, proj_b, oc, onorm, wa, wb, wc, nw, w_gate_up, w_gate_up, w_down)


def _rope_tables(seq):
    t = np.arange(seq, dtype=np.float64)

    def build(groups):
        cos = np.ones((seq, HEAD_DIM))
        s_lo = np.zeros((seq, HEAD_DIM))
        s_hi = np.zeros((seq, HEAD_DIM))
        for start, half, theta, pos in groups:
            inv = np.float64(np.float32(theta)) ** (-np.arange(half, dtype=np.float64) / half)
            ang = pos[:, None] * inv[None, :]
            cos[:, start:start + half] = np.cos(ang)
            cos[:, start + half:start + 2 * half] = np.cos(ang)
            s_lo[:, start:start + half] = -np.sin(ang)
            s_hi[:, start + half:start + 2 * half] = np.sin(ang)
        tab = np.stack([cos, s_lo, s_hi])
        return np.tile(tab, (1, 1, N_HEADS)).astype(np.float32)

    tab_a = build([(0, A_ROT_HALF, A_THETA, t)])
    row = np.floor(t / GRID_W)
    col = t - row * GRID_W
    tab_c = build([(0, C_ROT_HALF, C_THETA, row), (2 * C_ROT_HALF, C_ROT_HALF, C_THETA, col)])
    return tab_a, tab_c


def _multiplicity_tables(seq):
    (w0, d0), rest = A_PATTERNS[0], A_PATTERNS[1:]
    assert d0 == 1 and all(d % N_CLS == 0 for _, d in rest)
    r0 = w0 // 2
    tq, win = A_LOCAL_TQ, A_LOCAL_TQ + 2 * r0
    qi = np.arange(tq)[:, None]
    kj = np.arange(win)[None, :]
    local = np.stack([(np.abs(qi + off - kj) <= r0) for off in (0, r0, 2 * r0)]).astype(np.float32)
    n = seq // N_CLS
    d = np.arange(n)[:, None] - np.arange(n)[None, :]
    dilated = np.zeros((n, n), np.float32)
    for window, dil in rest:
        step = dil // N_CLS
        radius = window // (2 * dil)
        dilated += ((d % step == 0) & (np.abs(d) <= radius * step)).astype(np.float32)
    return local, dilated


def _head_mean_matrix():
    e = np.kron(np.eye(N_HEADS), np.full((HEAD_DIM, HEAD_DIM), 1.0 / HEAD_DIM))
    return e.astype(np.float32)


def _pad_lanes(v, width):
    return jnp.pad(v.reshape(1, -1), ((0, 0), (0, width - v.size)))


def kernel(x, norm1, w_in, qn_a, kn_a, conv_b, a_log_b, dt_bias_b, onorm_b, qn_c, kn_c, w_out, norm2, w_gate_up, w_down):
    batch, seq, d = x.shape
    depth = w_in.shape[0]
    t = batch * seq

    tab_a, tab_c = _rope_tables(seq)
    tab_a, tab_c = jnp.asarray(tab_a), jnp.asarray(tab_c)
    mult_local, mult_dil = _multiplicity_tables(seq)
    mult_local, mult_dil = jnp.asarray(mult_local, dtype=BF16), jnp.asarray(mult_dil, dtype=BF16)
    e_mat = jnp.asarray(_head_mean_matrix(), dtype=BF16)

    c0 = 3 * W_ATT
    c1 = c0 + 4 * W_GDN
    c2 = c1 + 4 * N_HEADS
    w_in_h = w_in.astype(BF16)
    cq = [w_in_h[:, :, c2 + h * HEAD_DIM:c2 + (h + 1) * HEAD_DIM] for h in C_HEAD_ORDER]
    w_in_p = jnp.concatenate(
        [w_in_h[:, :, c0:c1], w_in_h[:, :, 0:c0]] + cq + [w_in_h[:, :, c2 + W_ATT:], w_in_h[:, :, c1:c2],
         jnp.zeros(w_in.shape[:2] + (IN_PAD - w_in.shape[2],), BF16)], axis=-1)

    w_out_h = w_out.astype(BF16)
    w_out_a = w_out_h[:, 0:W_ATT]
    w_out_b = w_out_h[:, W_ATT:W_ATT + W_GDN]
    wc = w_out_h[:, W_ATT + W_GDN:]
    w_out_c = jnp.concatenate([wc[:, h * HEAD_DIM:(h + 1) * HEAD_DIM] for h in C_HEAD_ORDER], axis=1)
    w_gu = w_gate_up.astype(BF16)
    w_dn = w_down.astype(BF16)

    tile4 = lambda v: jnp.tile(v.reshape(1, -1), (1, N_HEADS))

    xt = x.reshape(t, d)
    for i in range(depth):
        proj, proj_ab, (qa_cls, ka_cls, va_cls), (qa, ka, va, qc, kc, vc) = _in_proj(
            xt, norm1[i].reshape(1, d), w_in_p[i], tile4(qn_a[i]), tile4(kn_a[i]), tile4(qn_c[i]), tile4(kn_c[i]),
            tab_a, tab_c, e_mat, batch, seq)
        o1, lse1 = _attn_a_local(qa, ka, va, mult_local, batch, seq)
        o_a = _attn_a_dilated(qa_cls, ka_cls, va_cls, mult_dil, o1, lse1, batch, seq)
        o_c = _attn_c(qc, kc, vc, batch, seq)
        qkv_b = _gdn_prep(proj, conv_b[i], batch, seq)
        alog, dtb = _pad_lanes(a_log_b[i], 128), _pad_lanes(dt_bias_b[i], 128)
        o_f, o_r = _gdn(qkv_b, proj_ab, alog, dtb, batch, seq)
        xt = _out_proj_ffn(xt, o_a, o_f, o_r, proj, o_c, onorm_b[i].reshape(1, B_DK),
                           w_out_a[i], w_out_b[i], w_out_c[i], norm2[i].reshape(1, d), w_gu[i], w_dn[i])
    return xt.reshape(batch, seq, d)
```
